```python
import functools
import jax, jax.numpy as jnp
from jax import lax
import numpy as np

D_MODEL = 1024
BATCH = 16
SEQ = 2048
DEPTH = 1
DEC_BATCH = 128
DEC_SEQ = 4
PAST_LEN = 16384
PAGE_SIZE = 128

V_DIM = 128
N_HEADS = D_MODEL // V_DIM
MLA_WIDTH = N_HEADS * V_DIM
Q_LORA = 384
KV_LORA = 256
NOPE_DIM = 128
ROPE_DIM = 64
ROPE_THETA = 10000.0
ATTN_SCALE = (NOPE_DIM + ROPE_DIM) ** -0.5
Q_BLOCK = 128
CHUNK = 128
SGU_WIDTH = D_MODEL
SGU_GROUP_WIDTH = 128
SGU_GROUPS = SGU_WIDTH // SGU_GROUP_WIDTH
_IN_PARTS = (Q_LORA, KV_LORA, ROPE_DIM, SGU_WIDTH, SGU_WIDTH, MLA_WIDTH, SGU_WIDTH)
IN_WIDTH = sum(_IN_PARTS)
IN_SPLITS = tuple(int(c) for c in np.cumsum(_IN_PARTS)[:-1])
N_GROUPS = 8
EXPERTS_PER_GROUP = 8
N_EXPERTS = N_GROUPS * EXPERTS_PER_GROUP
TOP_K = 2
D_EXPERT = 256
ROW_BLOCK = 128
EPS = 1e-6
POOL_NUM, POOL_DEN = 5, 4

kernel_name = "mla_sgu_gated_hier_moe_step"


def rmsnorm(x, g):
    xf = x.astype(jnp.float32)
    y = xf * lax.rsqrt(jnp.mean(xf * xf, axis=-1, keepdims=True) + EPS)
    return (y * g.astype(jnp.float32)).astype(x.dtype)


def rope(x, pos):
    half = ROPE_DIM // 2
    freq = ROPE_THETA ** (-jnp.arange(half, dtype=jnp.float32) / half)
    ang = pos.astype(jnp.float32)[:, None] * freq[None, :]
    cos = jnp.cos(ang)[None, :, None, :]
    sin = jnp.sin(ang)[None, :, None, :]
    xf = x.astype(jnp.float32)
    x1, x2 = xf[..., :half], xf[..., half:]
    return jnp.concatenate([x1 * cos - x2 * sin, x2 * cos + x1 * sin], axis=-1).astype(x.dtype)


def mla_project(z_q, z_kv, z_kr, pos, q_norm_g, kv_norm_g, w_uq, w_qr, w_uk):
    c_q = rmsnorm(z_q, q_norm_g)
    q_nope = jnp.einsum('bsr,rhd->bshd', c_q, w_uq)
    q_rope = rope(jnp.einsum('bsr,rhd->bshd', c_q, w_qr), pos)
    q_lat = jnp.einsum('bshd,chd->bshc', q_nope, w_uk)
    c_kv = rmsnorm(z_kv, kv_norm_g)
    k_rope = rope(z_kr[:, :, None, :], pos)[:, :, 0, :]
    return q_lat, q_rope, c_kv, k_rope


def mla_scores(q_lat, q_rope, c_kv, k_rope):
    s = (jnp.einsum('bqhc,bkc->bhqk', q_lat, c_kv, preferred_element_type=jnp.float32)
         + jnp.einsum('bqhr,bkr->bhqk', q_rope, k_rope, preferred_element_type=jnp.float32))
    return s * ATTN_SCALE


def prompt_attention(q_lat, q_rope, c_kv, k_rope):
    b, s, h, c = q_lat.shape
    nb = s // Q_BLOCK
    ql = jnp.moveaxis(q_lat.reshape(b, nb, Q_BLOCK, h, c), 1, 0)
    qr = jnp.moveaxis(q_rope.reshape(b, nb, Q_BLOCK, h, ROPE_DIM), 1, 0)
    kpos = jnp.arange(s, dtype=jnp.int32)

    def one_block(args):
        i, qlb, qrb = args
        qpos = i * Q_BLOCK + jnp.arange(Q_BLOCK, dtype=jnp.int32)
        sc = mla_scores(qlb, qrb, c_kv, k_rope)
        sc = jnp.where(kpos[None, :] <= qpos[:, None], sc, -jnp.inf)
        p = jax.nn.softmax(sc, axis=-1)
        return jnp.einsum('bhqk,bkc->bqhc', p, c_kv, preferred_element_type=jnp.float32)

    o = lax.map(one_block, (jnp.arange(nb, dtype=jnp.int32), ql, qr))
    return jnp.moveaxis(o, 0, 1).reshape(b, s, h, c).astype(q_lat.dtype)


def sample_attention(q_lat, q_rope, c_kv, k_rope, cache_ckv, cache_krope, page_table):
    t = q_lat.shape[1]
    sc = mla_scores(q_lat, q_rope, c_kv, k_rope)
    causal = jnp.tril(jnp.ones((t, t), dtype=bool))
    sc = jnp.where(causal, sc, -jnp.inf)
    m0 = jnp.max(sc, axis=-1)
    p0 = jnp.exp(sc - m0[..., None])
    l0 = jnp.sum(p0, axis=-1)
    acc0 = jnp.einsum('bhqk,bkc->bhqc', p0, c_kv, preferred_element_type=jnp.float32)

    def step(carry, pages):
        m, l, acc = carry
        ck = cache_ckv[pages]
        kr = cache_krope[pages]
        s = mla_scores(q_lat, q_rope, ck, kr)
        m_new = jnp.maximum(m, jnp.max(s, axis=-1))
        alpha = jnp.exp(m - m_new)
        p = jnp.exp(s - m_new[..., None])
        l = l * alpha + jnp.sum(p, axis=-1)
        acc = acc * alpha[..., None] + jnp.einsum('bhqk,bkc->bhqc', p, ck, preferred_element_type=jnp.float32)
        return (m_new, l, acc), None

    (m, l, acc), _ = lax.scan(step, (m0, l0, acc0), page_table.T)
    o = acc / l[..., None]
    return jnp.transpose(o, (0, 2, 1, 3)).astype(q_lat.dtype)


def spatial_gate(v, w_s, b_s, rows):
    b, s, _ = v.shape
    causal = jnp.tril(jnp.ones((rows, rows), dtype=bool))
    ws = jnp.where(causal[None], w_s[:, :rows, :rows], 0)
    vc = v.reshape(b, s // rows, rows, SGU_GROUPS, SGU_GROUP_WIDTH)
    mixed = jnp.einsum('gij,bnjgc->bnigc', ws, vc) + b_s[:, :rows].T[None, None, :, :, None]
    return mixed.reshape(b, s, SGU_WIDTH)


def token_mixer(x, pos, attend, rows, norm1_g, w_in, q_norm_g, kv_norm_g, w_uq, w_qr, w_uk, w_uv,
                sgu_norm_g, w_s, b_s, w_o):
    b, s, _ = x.shape
    z = rmsnorm(x, norm1_g) @ w_in
    z_q, z_kv, z_kr, z_u, z_v, z_ga, z_gb = jnp.split(z, IN_SPLITS, axis=-1)
    q_lat, q_rope, c_kv, k_rope = mla_project(z_q, z_kv, z_kr, pos, q_norm_g, kv_norm_g, w_uq, w_qr, w_uk)
    o_lat = attend(q_lat, q_rope, c_kv, k_rope)
    attn = jnp.einsum('bshc,che->bshe', o_lat, w_uv).reshape(b, s, MLA_WIDTH)
    u = jax.nn.gelu(z_u)
    v = rmsnorm(jax.nn.gelu(z_v), sgu_norm_g)
    sgu = u * spatial_gate(v, w_s, b_s, rows)
    merged = jax.nn.sigmoid(z_ga) * attn + jax.nn.sigmoid(z_gb) * sgu
    return x + merged @ w_o, c_kv, k_rope, v


def grouped_expert_ffn(xt, expert, weights, w_gate, w_up, w_down):
    t, d = xt.shape
    a = t * TOP_K
    flat_e = expert.reshape(a)
    order = jnp.argsort(flat_e)
    e_sorted = flat_e[order]
    tok_sorted = order // TOP_K
    w_sorted = weights.reshape(a)[order]
    sizes = jnp.bincount(flat_e, length=N_EXPERTS)
    start = jnp.cumsum(sizes) - sizes
    padded = (sizes + ROW_BLOCK - 1) // ROW_BLOCK * ROW_BLOCK
    pad_end = jnp.cumsum(padded)
    pad_start = pad_end - padded
    pos = pad_start[e_sorted] + (jnp.arange(a, dtype=jnp.int32) - start[e_sorted])
    n_blocks = -(-a // ROW_BLOCK) + N_EXPERTS
    xs = jnp.zeros((n_blocks * ROW_BLOCK, d), xt.dtype).at[pos].set(xt[tok_sorted])
    block_start = jnp.arange(n_blocks, dtype=jnp.int32) * ROW_BLOCK
    block_expert = jnp.minimum(jnp.searchsorted(pad_end, block_start, side='right'), N_EXPERTS - 1)

    def run_block(args):
        xb, e = args
        h = jax.nn.silu(xb @ w_gate[e]) * (xb @ w_up[e])
        return h @ w_down[e]

    ys = lax.map(run_block, (xs.reshape(n_blocks, ROW_BLOCK, d), block_expert)).reshape(-1, d)
    out = ys[pos] * w_sorted[:, None].astype(ys.dtype)
    return jnp.zeros_like(xt).at[tok_sorted].add(out.astype(xt.dtype))


def channel_mixer(x, norm2_g, w_rg, b_rg, w_re, b_re, w_gate, w_up, w_down):
    b, s, d = x.shape
    xt = rmsnorm(x, norm2_g).reshape(b * s, d)
    g_logit = (xt @ w_rg).astype(jnp.float32) + b_rg.astype(jnp.float32)
    g_w, g_idx = lax.top_k(jax.nn.softmax(g_logit, axis=-1), 1)
    e_logit = ((xt @ w_re).astype(jnp.float32) + b_re.astype(jnp.float32)).reshape(b * s, N_GROUPS, EXPERTS_PER_GROUP)
    e_logit = jnp.take_along_axis(e_logit, g_idx[:, :, None], axis=1)[:, 0]
    e_w, e_idx = lax.top_k(jax.nn.softmax(e_logit, axis=-1), TOP_K)
    e_w = e_w / jnp.sum(e_w, axis=-1, keepdims=True)
    weights = g_w * e_w
    expert = g_idx * EXPERTS_PER_GROUP + e_idx
    y = grouped_expert_ffn(xt, expert, weights, w_gate, w_up, w_down)
    return x + y.reshape(b, s, d)


def setup_inputs(seed: int = 0) -> dict:
    key = jax.random.key(seed)
    ks = jax.random.split(key, 32)
    f32 = jnp.float32
    L = DEPTH
    n_pages = PAST_LEN // PAGE_SIZE
    n_phys = (DEC_BATCH * n_pages * POOL_NUM) // POOL_DEN

    def nrm(k, shape, scale):
        return jax.random.normal(k, shape, f32) * scale

    def gain(k, shape):
        return 1.0 + 0.01 * jax.random.normal(k, shape, f32)

    page_table = jax.random.permutation(ks[4], n_phys)[: DEC_BATCH * n_pages].reshape(DEC_BATCH, n_pages).astype(jnp.int32)
    return {
        "x_prompt": nrm(ks[0], (BATCH, SEQ, D_MODEL), 1.0),
        "x_sample": nrm(ks[1], (DEC_BATCH, DEC_SEQ, D_MODEL), 1.0),
        "cache_ckv": nrm(ks[2], (L, n_phys, PAGE_SIZE, KV_LORA), 1.0),
        "cache_krope": nrm(ks[3], (L, n_phys, PAGE_SIZE, ROPE_DIM), 1.0),
        "page_table": page_table,
        "norm1_g": gain(ks[5], (L, D_MODEL)),
        "w_in": nrm(ks[6], (L, D_MODEL, IN_WIDTH), D_MODEL ** -0.5),
        "q_norm_g": gain(ks[7], (L, Q_LORA)),
        "kv_norm_g": gain(ks[8], (L, KV_LORA)),
        "w_uq": nrm(ks[9], (L, Q_LORA, N_HEADS, NOPE_DIM), Q_LORA ** -0.5),
        "w_qr": nrm(ks[10], (L, Q_LORA, N_HEADS, ROPE_DIM), Q_LORA ** -0.5),
        "w_uk": nrm(ks[11], (L, KV_LORA, N_HEADS, NOPE_DIM), KV_LORA ** -0.5),
        "w_uv": nrm(ks[12], (L, KV_LORA, N_HEADS, V_DIM), KV_LORA ** -0.5),
        "sgu_norm_g": gain(ks[13], (L, SGU_WIDTH)),
        "w_s": nrm(ks[14], (L, SGU_GROUPS, CHUNK, CHUNK), CHUNK ** -0.5),
        "b_s": gain(ks[15], (L, SGU_GROUPS, CHUNK)),
        "w_o": nrm(ks[16], (L, D_MODEL, D_MODEL), D_MODEL ** -0.5),
        "norm2_g": gain(ks[17], (L, D_MODEL)),
        "w_rg": nrm(ks[18], (L, D_MODEL, N_GROUPS), D_MODEL ** -0.5),
        "b_rg": nrm(ks[19], (L, N_GROUPS), 0.01),
        "w_re": nrm(ks[20], (L, D_MODEL, N_EXPERTS), D_MODEL ** -0.5),
        "b_re": nrm(ks[21], (L, N_EXPERTS), 0.01),
        "w_gate": nrm(ks[22], (L, N_EXPERTS, D_MODEL, D_EXPERT), D_MODEL ** -0.5),
        "w_up": nrm(ks[23], (L, N_EXPERTS, D_MODEL, D_EXPERT), D_MODEL ** -0.5),
        "w_down": nrm(ks[24], (L, N_EXPERTS, D_EXPERT, D_MODEL), D_EXPERT ** -0.5),
        "norm_f_g": gain(ks[25], (D_MODEL,)),
    }


def reference(x_prompt, x_sample, cache_ckv, cache_krope, page_table, norm1_g, w_in, q_norm_g, kv_norm_g,
              w_uq, w_qr, w_uk, w_uv, sgu_norm_g, w_s, b_s, w_o, norm2_g, w_rg, b_rg, w_re, b_re,
              w_gate, w_up, w_down, norm_f_g):
    b_p, s_p, _ = x_prompt.shape
    b_s_, t_s, _ = x_sample.shape
    past_len = page_table.shape[1] * PAGE_SIZE
    pos_p = jnp.arange(s_p, dtype=jnp.int32)
    pos_s = past_len + jnp.arange(t_s, dtype=jnp.int32)
    xp, xs = x_prompt, x_sample
    ckv_p, kr_p, ckv_s, kr_s, v_s = [], [], [], [], []
    for l in range(DEPTH):
        mix = (norm1_g[l], w_in[l], q_norm_g[l], kv_norm_g[l], w_uq[l], w_qr[l], w_uk[l], w_uv[l],
               sgu_norm_g[l], w_s[l], b_s[l], w_o[l])
        moe = (norm2_g[l], w_rg[l], b_rg[l], w_re[l], b_re[l], w_gate[l], w_up[l], w_down[l])
        xp, c_p, k_p, _ = token_mixer(xp, pos_p, prompt_attention, CHUNK, *mix)
        xp = channel_mixer(xp, *moe)
        attend_s = functools.partial(sample_attention, cache_ckv=cache_ckv[l], cache_krope=cache_krope[l],
                                     page_table=page_table)
        xs, c_s, k_s, vv = token_mixer(xs, pos_s, attend_s, t_s, *mix)
        xs = channel_mixer(xs, *moe)
        ckv_p.append(c_p.reshape(b_p, s_p // PAGE_SIZE, PAGE_SIZE, KV_LORA))
        kr_p.append(k_p.reshape(b_p, s_p // PAGE_SIZE, PAGE_SIZE, ROPE_DIM))
        ckv_s.append(c_s)
        kr_s.append(k_s)
        v_s.append(vv)
    y_prompt = rmsnorm(xp, norm_f_g)
    y_sample = rmsnorm(xs, norm_f_g)
    new_ckv_prompt = jnp.stack(ckv_p)
    new_krope_prompt = jnp.stack(kr_p)
    new_ckv_sample = jnp.stack(ckv_s)
    new_krope_sample = jnp.stack(kr_s)
    new_sgu_v_sample = jnp.stack(v_s)
    return (y_prompt, y_sample, new_ckv_prompt, new_krope_prompt, new_ckv_sample, new_krope_sample, new_sgu_v_sample)
```

```python
import functools
import math

import jax
import jax.numpy as jnp
from jax import lax
from jax.experimental import pallas as pl
from jax.experimental.pallas import tpu as pltpu

F32 = jnp.float32
BF16 = jnp.bfloat16

D_MODEL = 1024
PAGE_SIZE = 128
N_HEADS = 8
V_DIM = 128
Q_LORA = 384
KV_LORA = 256
NOPE_DIM = 128
ROPE_DIM = 64
ROPE_HALF = ROPE_DIM // 2
ROPE_THETA = 10000.0
ATTN_SCALE = (NOPE_DIM + ROPE_DIM) ** -0.5
CHUNK = 128
SGU_GROUPS = 8
SGU_GROUP_WIDTH = 128
N_GROUPS = 8
EXPERTS_PER_GROUP = 8
N_EXPERTS = N_GROUPS * EXPERTS_PER_GROUP
TOP_K = 2
D_EXPERT = 256
ROW_BLOCK = 128
EPS = 1e-6

LANES = 128
ROW_TILE = 256
Q_TOKENS = 128
PAGES_PER_STEP = 8
VMEM_LIMIT = 48 * 1024 * 1024
SMALL_IN = Q_LORA + KV_LORA + 2 * LANES
NEG_INF = float("-inf")


def _rms(x, g):
    return x * lax.rsqrt(jnp.mean(x * x, axis=-1, keepdims=True) + EPS) * g


def _dot(a, b):
    return jnp.dot(a, b, preferred_element_type=F32)


def _dot_nt(a, b):
    return lax.dot_general(a, b, (((1,), (1,)), ((), ())), preferred_element_type=F32)


def _const_spec(shape):
    nd = len(shape)
    return pl.BlockSpec(shape, lambda *_: (0,) * nd)


def _rope_table_kernel(cos_ref, sin_ref, *, seq, past_len, dec_seq):
    n = cos_ref.shape[0]
    row = lax.broadcasted_iota(jnp.int32, (n, LANES), 0)
    lane = lax.broadcasted_iota(jnp.int32, (n, LANES), 1)
    pos = jnp.where(row < seq, row, past_len + (row - seq) % dec_seq)
    j = (lane % ROPE_HALF).astype(F32)
    freq = jnp.exp(j * (-math.log(ROPE_THETA) / ROPE_HALF))
    ang = pos.astype(F32) * freq
    cos_ref[...] = jnp.cos(ang)
    sin_ref[...] = jnp.sin(ang)


def _rope_tables(seq, past_len, dec_seq):
    n = seq + ROW_TILE
    return pl.pallas_call(
        functools.partial(_rope_table_kernel, seq=seq, past_len=past_len, dec_seq=dec_seq),
        out_shape=(jax.ShapeDtypeStruct((n, LANES), F32), jax.ShapeDtypeStruct((n, LANES), F32)),
        name="rope_table",
    )()


def _proj_kernel(x_ref, cos_ref, sin_ref, n1_ref, win_ref, qn_ref, kvn_ref, wuq_ref, wqr_ref, wuk_ref,
                 ql_ref, qr_ref, ckv_ref, kr_ref, ckvb_ref, krb_ref):
    xn = _rms(x_ref[...], n1_ref[...]).astype(BF16)
    z = _dot(xn, win_ref[...])
    c_q = _rms(z[:, :Q_LORA], qn_ref[...]).astype(BF16)
    c_kv = _rms(z[:, Q_LORA:Q_LORA + KV_LORA], kvn_ref[...])
    cos = cos_ref[...]
    sin = sin_ref[...]
    kr0 = Q_LORA + KV_LORA
    k_rope = z[:, kr0:kr0 + LANES] * cos + z[:, kr0 + LANES:kr0 + 2 * LANES] * sin
    ckv_ref[...] = c_kv
    kr_ref[...] = k_rope[:, :ROPE_DIM]
    ckvb_ref[...] = c_kv.astype(BF16)
    krb_ref[...] = k_rope.astype(BF16)
    q_nope = _dot(c_q, wuq_ref[...]).astype(BF16)
    qr2 = _dot(c_q, wqr_ref[...])
    hw = N_HEADS * LANES
    cos8 = jnp.concatenate([cos] * N_HEADS, axis=1)
    sin8 = jnp.concatenate([sin] * N_HEADS, axis=1)
    q_rope = qr2[:, :hw] * cos8 + qr2[:, hw:] * sin8
    qr_ref[...] = (q_rope * ATTN_SCALE).astype(BF16)
    for h in range(N_HEADS):
        q_lat = _dot(q_nope[:, h * NOPE_DIM:(h + 1) * NOPE_DIM], wuk_ref[h])
        ql_ref[:, h * KV_LORA:(h + 1) * KV_LORA] = (q_lat * ATTN_SCALE).astype(BF16)


def _proj(x, cos_t, sin_t, table_block, p):
    rows = x.shape[0]
    nt = rows // ROW_TILE
    row_spec = lambda w: pl.BlockSpec((ROW_TILE, w), lambda i: (i, 0))
    tab_spec = pl.BlockSpec((ROW_TILE, LANES), lambda i: (table_block(i), 0))
    return pl.pallas_call(
        _proj_kernel,
        grid=(nt,),
        in_specs=[row_spec(D_MODEL), tab_spec, tab_spec,
                  _const_spec((1, D_MODEL)), _const_spec((D_MODEL, SMALL_IN)),
                  _const_spec((1, Q_LORA)), _const_spec((1, KV_LORA)),
                  _const_spec((Q_LORA, N_HEADS * NOPE_DIM)), _const_spec((Q_LORA, 2 * N_HEADS * LANES)),
                  _const_spec((N_HEADS, NOPE_DIM, KV_LORA))],
        out_specs=[row_spec(N_HEADS * KV_LORA), row_spec(N_HEADS * LANES), row_spec(KV_LORA),
                   row_spec(ROPE_DIM), row_spec(KV_LORA), row_spec(LANES)],
        out_shape=[jax.ShapeDtypeStruct((rows, N_HEADS * KV_LORA), BF16),
                   jax.ShapeDtypeStruct((rows, N_HEADS * LANES), BF16),
                   jax.ShapeDtypeStruct((rows, KV_LORA), F32),
                   jax.ShapeDtypeStruct((rows, ROPE_DIM), F32),
                   jax.ShapeDtypeStruct((rows, KV_LORA), BF16),
                   jax.ShapeDtypeStruct((rows, LANES), BF16)],
        compiler_params=pltpu.CompilerParams(dimension_semantics=("parallel",), vmem_limit_bytes=VMEM_LIMIT),
        name="in_proj",
    )(x, cos_t, sin_t, p["norm1_g"], p["w_in_small"], p["q_norm_g"], p["kv_norm_g"], p["w_uq"], p["w_qr2"],
      p["w_uk_t"])


def _prompt_attn_kernel(ql_ref, qr_ref, ckv_ref, kr_ref, o_ref, q_s, qr_s, acc_ref, m_ref, l_ref):
    qi = pl.program_id(1)
    for h in range(N_HEADS):
        q_s[h * Q_TOKENS:(h + 1) * Q_TOKENS, :] = ql_ref[:, h * KV_LORA:(h + 1) * KV_LORA]
        qr_s[h * Q_TOKENS:(h + 1) * Q_TOKENS, :] = qr_ref[:, h * LANES:(h + 1) * LANES]
    m_ref[...] = jnp.full(m_ref.shape, NEG_INF, F32)
    l_ref[...] = jnp.zeros(l_ref.shape, F32)
    acc_ref[...] = jnp.zeros(acc_ref.shape, F32)

    def step(j, masked):
        off = pl.multiple_of(j * Q_TOKENS, Q_TOKENS)
        k = ckv_ref[pl.ds(off, Q_TOKENS), :]
        kr = kr_ref[pl.ds(off, Q_TOKENS), :]
        s = _dot_nt(q_s[...], k) + _dot_nt(qr_s[...], kr)
        if masked:
            row = lax.broadcasted_iota(jnp.int32, s.shape, 0)
            col = lax.broadcasted_iota(jnp.int32, s.shape, 1)
            s = jnp.where(col <= row % Q_TOKENS, s, NEG_INF)
        m_prev = m_ref[...]
        m_new = jnp.maximum(m_prev, jnp.max(s, axis=-1, keepdims=True))
        alpha = jnp.exp(m_prev - m_new)
        p = jnp.exp(s - m_new)
        l_ref[...] = alpha * l_ref[...] + jnp.sum(p, axis=-1, keepdims=True)
        acc_ref[...] = acc_ref[...] * jnp.concatenate([alpha, alpha], axis=1) + _dot(p.astype(BF16), k)
        m_ref[...] = m_new

    def body(j, carry):
        step(j, False)
        return carry

    lax.fori_loop(0, qi, body, 0)
    step(qi, True)
    inv = 1.0 / l_ref[...]
    o = acc_ref[...] * jnp.concatenate([inv, inv], axis=1)
    for h in range(N_HEADS):
        o_ref[:, h * KV_LORA:(h + 1) * KV_LORA] = o[h * Q_TOKENS:(h + 1) * Q_TOKENS, :].astype(BF16)


def _prompt_attn(q_lat, q_rope, ckv_b, kr_b, batch, seq):
    nq = seq // Q_TOKENS
    hq = N_HEADS * Q_TOKENS
    return pl.pallas_call(
        _prompt_attn_kernel,
        grid=(batch, nq),
        in_specs=[pl.BlockSpec((Q_TOKENS, N_HEADS * KV_LORA), lambda b, i: (b * nq + i, 0)),
                  pl.BlockSpec((Q_TOKENS, N_HEADS * LANES), lambda b, i: (b * nq + i, 0)),
                  pl.BlockSpec((seq, KV_LORA), lambda b, i: (b, 0)),
                  pl.BlockSpec((seq, LANES), lambda b, i: (b, 0))],
        out_specs=pl.BlockSpec((Q_TOKENS, N_HEADS * KV_LORA), lambda b, i: (b * nq + i, 0)),
        out_shape=jax.ShapeDtypeStruct((batch * seq, N_HEADS * KV_LORA), BF16),
        scratch_shapes=[pltpu.VMEM((hq, KV_LORA), BF16), pltpu.VMEM((hq, LANES), BF16),
                        pltpu.VMEM((hq, KV_LORA), F32), pltpu.VMEM((hq, LANES), F32),
                        pltpu.VMEM((hq, LANES), F32)],
        compiler_params=pltpu.CompilerParams(dimension_semantics=("parallel", "arbitrary"),
                                             vmem_limit_bytes=VMEM_LIMIT),
        name="prompt_attn",
    )(q_lat, q_rope, ckv_b, kr_b)


def _sample_attn_kernel(pt_ref, q_ref, qr_ref, ckn_ref, krn_ref, *rest, dec_seq):
    del pt_ref
    npg = PAGES_PER_STEP
    ck_refs, kr_refs = rest[:npg], rest[npg:2 * npg]
    o_ref, acc_ref, m_ref, l_ref = rest[2 * npg:]
    j = pl.program_id(1)
    q = q_ref[...]
    qr = qr_ref[...]

    @pl.when(j == 0)
    def _():
        kn = ckn_ref[...]
        s = _dot_nt(q, kn) + _dot_nt(qr, krn_ref[...])
        row = lax.broadcasted_iota(jnp.int32, s.shape, 0)
        col = lax.broadcasted_iota(jnp.int32, s.shape, 1)
        s = jnp.where(col <= row % dec_seq, s, NEG_INF)
        m0 = jnp.max(s, axis=-1, keepdims=True)
        p = jnp.exp(s - m0)
        m_ref[...] = jnp.broadcast_to(m0, m_ref.shape)
        l_ref[...] = jnp.broadcast_to(jnp.sum(p, axis=-1, keepdims=True), l_ref.shape)
        acc_ref[...] = _dot(p.astype(BF16), kn)

    ks = [r[...].astype(BF16) for r in ck_refs]
    ss = [_dot_nt(q, k) + _dot_nt(qr, r[...].astype(BF16)) for k, r in zip(ks, kr_refs)]
    m_prev = m_ref[...]
    m_cur = functools.reduce(jnp.maximum, [jnp.max(s, axis=-1, keepdims=True) for s in ss])
    m_new = jnp.maximum(m_prev, m_cur)
    alpha = jnp.exp(m_prev - m_new)
    ps = [jnp.exp(s - m_new) for s in ss]
    l_ref[...] = alpha * l_ref[...] + sum(jnp.sum(p, axis=-1, keepdims=True) for p in ps)
    pv = sum(_dot(p.astype(BF16), k) for p, k in zip(ps, ks))
    acc_ref[...] = acc_ref[...] * jnp.concatenate([alpha, alpha], axis=1) + pv
    m_ref[...] = m_new

    @pl.when(j == pl.num_programs(1) - 1)
    def _():
        inv = 1.0 / l_ref[...]
        o_ref[...] = (acc_ref[...] * jnp.concatenate([inv, inv], axis=1)).astype(BF16)


def _sample_attn(page_table, q, qr, ck_new, kr_new, cache_ckv, cache_krope, dec_seq):
    batch, n_pages = page_table.shape
    npg = PAGES_PER_STEP
    steps = n_pages // npg
    rows = N_HEADS * dec_seq

    def page_spec(width, i):
        return pl.BlockSpec((None, PAGE_SIZE, width),
                            lambda b, j, pt: (pt[b * n_pages + j * npg + i], 0, 0))

    per_b = lambda w, r: pl.BlockSpec((None, r, w), lambda b, j, pt: (b, 0, 0))
    grid_spec = pltpu.PrefetchScalarGridSpec(
        num_scalar_prefetch=1,
        grid=(batch, steps),
        in_specs=[per_b(KV_LORA, rows), per_b(ROPE_DIM, rows), per_b(KV_LORA, PAGE_SIZE), per_b(ROPE_DIM, PAGE_SIZE)]
                 + [page_spec(KV_LORA, i) for i in range(npg)]
                 + [page_spec(ROPE_DIM, i) for i in range(npg)],
        out_specs=per_b(KV_LORA, rows),
        scratch_shapes=[pltpu.VMEM((rows, KV_LORA), F32), pltpu.VMEM((rows, LANES), F32),
                        pltpu.VMEM((rows, LANES), F32)],
    )
    return pl.pallas_call(
        functools.partial(_sample_attn_kernel, dec_seq=dec_seq),
        grid_spec=grid_spec,
        out_shape=jax.ShapeDtypeStruct((batch, rows, KV_LORA), BF16),
        compiler_params=pltpu.CompilerParams(dimension_semantics=("parallel", "arbitrary"),
                                             vmem_limit_bytes=VMEM_LIMIT),
        name="sample_attn",
    )(page_table.reshape(-1), q, qr, ck_new, kr_new, *([cache_ckv] * npg), *([cache_krope] * npg))


def _route(logits):
    lane = lax.broadcasted_iota(jnp.int32, logits.shape, 1)
    big = jnp.int32(LANES)
    gl = jnp.where(lane < N_GROUPS, logits, NEG_INF)
    gmax = jnp.max(gl, axis=-1, keepdims=True)
    gsum = jnp.sum(jnp.exp(gl - gmax), axis=-1, keepdims=True)
    g_w = 1.0 / gsum
    g_idx = jnp.min(jnp.where(gl == gmax, lane, big), axis=-1, keepdims=True)
    lo = N_GROUPS + g_idx * EXPERTS_PER_GROUP
    el = jnp.where((lane >= lo) & (lane < lo + EXPERTS_PER_GROUP), logits, NEG_INF)
    m1 = jnp.max(el, axis=-1, keepdims=True)
    i1 = jnp.min(jnp.where(el == m1, lane, big), axis=-1, keepdims=True)
    el2 = jnp.where(lane == i1, NEG_INF, el)
    m2 = jnp.max(el2, axis=-1, keepdims=True)
    i2 = jnp.min(jnp.where(el2 == m2, lane, big), axis=-1, keepdims=True)
    esum = jnp.sum(jnp.exp(el - m1), axis=-1, keepdims=True)
    p1 = 1.0 / esum
    p2 = jnp.exp(m2 - m1) / esum
    tot = p1 + p2
    return i1 - N_GROUPS, i2 - N_GROUPS, g_w * (p1 / tot), g_w * (p2 / tot)


def _post_kernel(x_ref, o_ref, n1_ref, win_ref, sg_ref, wuv_ref, ws_ref, bt_ref, wo_ref, n2_ref,
                 wrh_ref, wrl_ref, br_ref, x1_ref, route_ref, *v_out, chunk_rows):
    x = x_ref[...]
    xn = _rms(x, n1_ref[...]).astype(BF16)
    z = _dot(xn, win_ref[...])
    u = jax.nn.gelu(z[:, :D_MODEL])
    v = _rms(jax.nn.gelu(z[:, D_MODEL:2 * D_MODEL]), sg_ref[...])
    if v_out:
        v_out[0][...] = v
    ga = jax.nn.sigmoid(z[:, 2 * D_MODEL:3 * D_MODEL])
    gb = jax.nn.sigmoid(z[:, 3 * D_MODEL:])
    attn = jnp.concatenate(
        [_dot(o_ref[:, h * KV_LORA:(h + 1) * KV_LORA], wuv_ref[h]) for h in range(N_HEADS)], axis=1)
    row = lax.broadcasted_iota(jnp.int32, (CHUNK, CHUNK), 0)
    col = lax.broadcasted_iota(jnp.int32, (CHUNK, CHUNK), 1)
    keep = (col <= row) & (row // chunk_rows == col // chunk_rows)
    vb = v.astype(BF16)
    bt = bt_ref[...]
    mixed_rows = []
    for c in range(ROW_TILE // CHUNK):
        groups = []
        for g in range(SGU_GROUPS):
            ws = jnp.where(keep, ws_ref[g], 0.0).astype(BF16)
            vg = vb[c * CHUNK:(c + 1) * CHUNK, g * SGU_GROUP_WIDTH:(g + 1) * SGU_GROUP_WIDTH]
            groups.append(_dot(ws, vg) + bt[:, g:g + 1])
        mixed_rows.append(jnp.concatenate(groups, axis=1))
    mixed = jnp.concatenate(mixed_rows, axis=0)
    merged = ga * attn + gb * (u * mixed)
    x1 = x + _dot(merged.astype(BF16), wo_ref[...])
    x1_ref[...] = x1
    xt = _rms(x1, n2_ref[...])
    xt_hi = xt.astype(BF16)
    xt_lo = (xt - xt_hi.astype(F32)).astype(BF16)
    logits = (_dot(xt_hi, wrh_ref[...]) + _dot(xt_lo, wrh_ref[...]) + _dot(xt_hi, wrl_ref[...])) + br_ref[...]
    e0, e1, w0, w1 = _route(logits)
    lane = lax.broadcasted_iota(jnp.int32, logits.shape, 1)
    route_ref[...] = jnp.where(lane == 0, e0.astype(F32),
                               jnp.where(lane == 1, e1.astype(F32),
                                         jnp.where(lane == 2, w0, jnp.where(lane == 3, w1, 0.0))))


def _post(x, o_lat, ws_eff, bt_eff, chunk_rows, emit_v, p):
    rows = x.shape[0]
    nt = rows // ROW_TILE
    row_spec = lambda w: pl.BlockSpec((ROW_TILE, w), lambda i: (i, 0))
    out_specs = [row_spec(D_MODEL), row_spec(LANES)]
    out_shape = [jax.ShapeDtypeStruct((rows, D_MODEL), F32), jax.ShapeDtypeStruct((rows, LANES), F32)]
    if emit_v:
        out_specs.append(row_spec(D_MODEL))
        out_shape.append(jax.ShapeDtypeStruct((rows, D_MODEL), F32))
    return pl.pallas_call(
        functools.partial(_post_kernel, chunk_rows=chunk_rows),
        grid=(nt,),
        in_specs=[row_spec(D_MODEL), row_spec(N_HEADS * KV_LORA),
                  _const_spec((1, D_MODEL)), _const_spec((D_MODEL, 4 * D_MODEL)), _const_spec((1, D_MODEL)),
                  _const_spec((N_HEADS, KV_LORA, V_DIM)), _const_spec((SGU_GROUPS, CHUNK, CHUNK)),
                  _const_spec((CHUNK, SGU_GROUPS)), _const_spec((D_MODEL, D_MODEL)), _const_spec((1, D_MODEL)),
                  _const_spec((D_MODEL, LANES)), _const_spec((D_MODEL, LANES)), _const_spec((1, LANES))],
        out_specs=out_specs,
        out_shape=out_shape,
        compiler_params=pltpu.CompilerParams(dimension_semantics=("parallel",), vmem_limit_bytes=VMEM_LIMIT),
        name="merge_route",
    )(x, o_lat, p["norm1_g"], p["w_in_big"], p["sgu_norm_g"], p["w_uv_h"], ws_eff, bt_eff, p["w_o"],
      p["norm2_g"], p["w_r_hi"], p["w_r_lo"], p["b_r"])


def _rank_kernel(route_ref, rank_ref, counts_ref, carry_ref):
    @pl.when(pl.program_id(0) == 0)
    def _():
        carry_ref[...] = jnp.zeros(carry_ref.shape, F32)

    r = route_ref[...]
    lane = lax.broadcasted_iota(jnp.int32, r.shape, 1)
    oh0 = jnp.where(lane == r[:, 0:1].astype(jnp.int32), 1.0, 0.0)
    oh1 = jnp.where(lane == r[:, 1:2].astype(jnp.int32), 1.0, 0.0)
    both = oh0 + oh1
    row = lax.broadcasted_iota(jnp.int32, (ROW_TILE, ROW_TILE), 0)
    col = lax.broadcasted_iota(jnp.int32, (ROW_TILE, ROW_TILE), 1)
    before = jnp.where(col < row, 1.0, 0.0).astype(BF16)
    base = carry_ref[...] + _dot(before, both.astype(BF16))
    rank0 = jnp.sum(oh0 * base, axis=-1, keepdims=True)
    rank1 = jnp.sum(oh1 * (base + oh0), axis=-1, keepdims=True)
    rank_ref[...] = jnp.where(lane == 0, rank0, jnp.where(lane == 1, rank1, 0.0))
    carry_ref[...] = carry_ref[...] + jnp.sum(both, axis=0, keepdims=True)
    counts_ref[...] = carry_ref[...]


def _rank(route):
    rows = route.shape[0]
    return pl.pallas_call(
        _rank_kernel,
        grid=(rows // ROW_TILE,),
        in_specs=[pl.BlockSpec((ROW_TILE, LANES), lambda i: (i, 0))],
        out_specs=[pl.BlockSpec((ROW_TILE, LANES), lambda i: (i, 0)), _const_spec((1, LANES))],
        out_shape=[jax.ShapeDtypeStruct((rows, LANES), F32), jax.ShapeDtypeStruct((1, LANES), F32)],
        scratch_shapes=[pltpu.VMEM((1, LANES), F32)],
        compiler_params=pltpu.CompilerParams(dimension_semantics=("arbitrary",)),
        name="expert_rank",
    )(route)


def _row_copy(src, src_row, dst, dst_row, sem):
    return pltpu.make_async_copy(src.at[pl.ds(src_row, 1)], dst.at[pl.ds(dst_row, 1)], sem)


def _dispatch_kernel(pos_ref, x1_ref, n2_ref, xs_in_ref, xs_ref, xt_ref, sem):
    del xs_in_ref
    xt_ref[...] = _rms(x1_ref[...], n2_ref[...])

    def issue(r, carry):
        for k in range(TOP_K):
            _row_copy(xt_ref, r, xs_ref, pos_ref[0, 0, TOP_K * r + k], sem).start()
        return carry

    lax.fori_loop(0, ROW_TILE, issue, 0)

    def drain(r, carry):
        for k in range(TOP_K):
            _row_copy(xt_ref, 0, xs_ref, 0, sem).wait()
        return carry

    lax.fori_loop(0, ROW_TILE, drain, 0)


def _dispatch(pos3, x1, norm2_g, xs):
    rows = x1.shape[0]
    return pl.pallas_call(
        _dispatch_kernel,
        grid=(rows // ROW_TILE,),
        in_specs=[pl.BlockSpec((1, 1, TOP_K * ROW_TILE), lambda i: (i, 0, 0), memory_space=pltpu.SMEM),
                  pl.BlockSpec((ROW_TILE, D_MODEL), lambda i: (i, 0)),
                  _const_spec((1, D_MODEL)),
                  pl.BlockSpec(memory_space=pl.ANY)],
        out_specs=pl.BlockSpec(memory_space=pl.ANY),
        out_shape=jax.ShapeDtypeStruct(xs.shape, xs.dtype),
        scratch_shapes=[pltpu.VMEM((ROW_TILE, D_MODEL), F32), pltpu.SemaphoreType.DMA(())],
        input_output_aliases={3: 0},
        compiler_params=pltpu.CompilerParams(dimension_semantics=("arbitrary",), has_side_effects=True),
        name="dispatch",
    )(pos3, x1, norm2_g, xs)


def _expert_kernel(be_ref, nb_ref, xs_ref, wg_ref, wu_ref, wd_ref, ys_ref, wg_s, wu_s, wd_s):
    i = pl.program_id(0)
    prev = be_ref[jnp.maximum(i - 1, 0)]

    @pl.when((i == 0) | (be_ref[i] != prev))
    def _():
        wg_s[...] = wg_ref[...].astype(BF16)
        wu_s[...] = wu_ref[...].astype(BF16)
        wd_s[...] = wd_ref[...].astype(BF16)

    @pl.when(i < nb_ref[0])
    def _():
        xb = xs_ref[...].astype(BF16)
        h = jax.nn.silu(_dot(xb, wg_s[...])) * _dot(xb, wu_s[...])
        ys_ref[...] = _dot(h.astype(BF16), wd_s[...])

    @pl.when(i >= nb_ref[0])
    def _():
        ys_ref[...] = jnp.zeros(ys_ref.shape, F32)


def _experts(block_expert, n_used, xs, w_gate, w_up, w_down):
    n_blocks = xs.shape[0] // ROW_BLOCK
    grid_spec = pltpu.PrefetchScalarGridSpec(
        num_scalar_prefetch=2,
        grid=(n_blocks,),
        in_specs=[pl.BlockSpec((ROW_BLOCK, D_MODEL), lambda i, be, nb: (i, 0)),
                  pl.BlockSpec((None, D_MODEL, D_EXPERT), lambda i, be, nb: (be[i], 0, 0)),
                  pl.BlockSpec((None, D_MODEL, D_EXPERT), lambda i, be, nb: (be[i], 0, 0)),
                  pl.BlockSpec((None, D_EXPERT, D_MODEL), lambda i, be, nb: (be[i], 0, 0))],
        out_specs=pl.BlockSpec((ROW_BLOCK, D_MODEL), lambda i, be, nb: (i, 0)),
        scratch_shapes=[pltpu.VMEM((D_MODEL, D_EXPERT), BF16), pltpu.VMEM((D_MODEL, D_EXPERT), BF16),
                        pltpu.VMEM((D_EXPERT, D_MODEL), BF16)],
    )
    return pl.pallas_call(
        _expert_kernel,
        grid_spec=grid_spec,
        out_shape=jax.ShapeDtypeStruct(xs.shape, F32),
        compiler_params=pltpu.CompilerParams(dimension_semantics=("arbitrary",), vmem_limit_bytes=VMEM_LIMIT),
        name="expert_ffn",
    )(block_expert, n_used, xs, w_gate, w_up, w_down)


def _combine_kernel(pos_ref, x1_ref, route_ref, nf_ref, ys_ref, y_ref, buf_ref, sem):
    def issue(r, carry):
        for k in range(TOP_K):
            _row_copy(ys_ref, pos_ref[0, 0, TOP_K * r + k], buf_ref.at[k], r, sem).start()
        return carry

    lax.fori_loop(0, ROW_TILE, issue, 0)

    def drain(r, carry):
        for k in range(TOP_K):
            _row_copy(ys_ref, 0, buf_ref.at[k], 0, sem).wait()
        return carry

    lax.fori_loop(0, ROW_TILE, drain, 0)
    route = route_ref[...]
    y = x1_ref[...] + buf_ref[0] * route[:, 2:3] + buf_ref[1] * route[:, 3:4]
    y_ref[...] = _rms(y, nf_ref[...])


def _combine(pos3, x1, route, norm_f_g, ys):
    rows = x1.shape[0]
    return pl.pallas_call(
        _combine_kernel,
        grid=(rows // ROW_TILE,),
        in_specs=[pl.BlockSpec((1, 1, TOP_K * ROW_TILE), lambda i: (i, 0, 0), memory_space=pltpu.SMEM),
                  pl.BlockSpec((ROW_TILE, D_MODEL), lambda i: (i, 0)),
                  pl.BlockSpec((ROW_TILE, LANES), lambda i: (i, 0)),
                  _const_spec((1, D_MODEL)),
                  pl.BlockSpec(memory_space=pl.ANY)],
        out_specs=pl.BlockSpec((ROW_TILE, D_MODEL), lambda i: (i, 0)),
        out_shape=jax.ShapeDtypeStruct((rows, D_MODEL), F32),
        scratch_shapes=[pltpu.VMEM((TOP_K, ROW_TILE, D_MODEL), F32), pltpu.SemaphoreType.DMA(())],
        compiler_params=pltpu.CompilerParams(dimension_semantics=("arbitrary",)),
        name="combine",
    )(pos3, x1, route, norm_f_g, ys)


def _prepare_params(norm1_g, w_in, q_norm_g, kv_norm_g, w_uq, w_qr, w_uk, w_uv, sgu_norm_g, w_o, norm2_g,
                    w_rg, b_rg, w_re, b_re):
    def rot(w):
        return jnp.concatenate([-w[..., ROPE_HALF:], w[..., :ROPE_HALF]], axis=-1)

    q0, kv0, kr0 = 0, Q_LORA, Q_LORA + KV_LORA
    big0 = kr0 + ROPE_DIM
    w_kr = w_in[:, kr0:big0]
    w_in_small = jnp.concatenate([w_in[:, q0:kr0], w_kr, w_kr, rot(w_kr), rot(w_kr)], axis=1).astype(BF16)
    zeros = jnp.zeros((Q_LORA, N_HEADS, LANES - ROPE_DIM), w_qr.dtype)
    pad = lambda w: jnp.concatenate([w, zeros], axis=-1).reshape(Q_LORA, N_HEADS * LANES)
    w_qr2 = jnp.concatenate([pad(w_qr), pad(rot(w_qr))], axis=1).astype(BF16)
    w_r = jnp.concatenate([w_rg, w_re, jnp.zeros((D_MODEL, LANES - N_GROUPS - N_EXPERTS), F32)], axis=1)
    w_r_hi = w_r.astype(BF16)
    b_r = jnp.concatenate([b_rg, b_re, jnp.zeros((LANES - N_GROUPS - N_EXPERTS,), F32)])[None, :]
    return dict(
        norm1_g=norm1_g[None, :], w_in_small=w_in_small, w_in_big=w_in[:, big0:].astype(BF16),
        q_norm_g=q_norm_g[None, :], kv_norm_g=kv_norm_g[None, :],
        w_uq=w_uq.reshape(Q_LORA, N_HEADS * NOPE_DIM).astype(BF16), w_qr2=w_qr2,
        w_uk_t=jnp.transpose(w_uk, (1, 2, 0)).astype(BF16), w_uv_h=jnp.transpose(w_uv, (1, 0, 2)).astype(BF16),
        sgu_norm_g=sgu_norm_g[None, :], w_o=w_o.astype(BF16), norm2_g=norm2_g[None, :],
        w_r_hi=w_r_hi, w_r_lo=(w_r - w_r_hi.astype(F32)).astype(BF16), b_r=b_r)


def kernel(x_prompt, x_sample, cache_ckv, cache_krope, page_table, norm1_g, w_in, q_norm_g, kv_norm_g, w_uq, w_qr,
           w_uk, w_uv, sgu_norm_g, w_s, b_s, w_o, norm2_g, w_rg, b_rg, w_re, b_re, w_gate, w_up, w_down, norm_f_g):
    b_p, s_p, _ = x_prompt.shape
    b_d, t_d, _ = x_sample.shape
    depth = norm1_g.shape[0]
    past_len = page_table.shape[1] * PAGE_SIZE
    rows_p, rows_d = b_p * s_p, b_d * t_d
    assert depth == 1 and s_p % ROW_TILE == 0 and rows_d % ROW_TILE == 0 and CHUNK % t_d == 0
    cos_t, sin_t = _rope_tables(s_p, past_len, t_d)
    tiles_per_seq = s_p // ROW_TILE

    xp = x_prompt.reshape(rows_p, D_MODEL)
    xd = x_sample.reshape(rows_d, D_MODEL)
    l = 0
    p = _prepare_params(norm1_g[l], w_in[l], q_norm_g[l], kv_norm_g[l], w_uq[l], w_qr[l], w_uk[l], w_uv[l],
                        sgu_norm_g[l], w_o[l], norm2_g[l], w_rg[l], b_rg[l], w_re[l], b_re[l])

    ql_p, qr_p, ckv_p, kr_p, ckvb_p, krb_p = _proj(xp, cos_t, sin_t, lambda i: i % tiles_per_seq, p)
    o_p = _prompt_attn(ql_p, qr_p, ckvb_p, krb_p, b_p, s_p)
    bt_p = jnp.transpose(b_s[l])
    x1_p, route_p = _post(xp, o_p, w_s[l], bt_p, CHUNK, False, p)

    ql_d, qr_d, ckv_d, kr_d, ckvb_d, krb_d = _proj(xd, cos_t, sin_t, lambda i: tiles_per_seq, p)
    hr = N_HEADS * t_d
    q_d = ql_d.reshape(b_d, t_d, N_HEADS, KV_LORA).transpose(0, 2, 1, 3).reshape(b_d, hr, KV_LORA)
    qrope_d = qr_d.reshape(b_d, t_d, N_HEADS, LANES)[..., :ROPE_DIM].transpose(0, 2, 1, 3).reshape(b_d, hr, ROPE_DIM)
    pad_rows = lambda a: jnp.pad(a.reshape(b_d, t_d, -1), ((0, 0), (0, PAGE_SIZE - t_d), (0, 0)))
    o_d = _sample_attn(page_table, q_d, qrope_d, pad_rows(ckvb_d), pad_rows(krb_d[:, :ROPE_DIM]),
                       cache_ckv[l], cache_krope[l], t_d)
    o_d = o_d.reshape(b_d, N_HEADS, t_d, KV_LORA).transpose(0, 2, 1, 3).reshape(rows_d, N_HEADS * KV_LORA)
    reps = CHUNK // t_d
    ws_d = jnp.tile(w_s[l][:, :t_d, :t_d], (1, reps, reps))
    bt_d = jnp.transpose(jnp.tile(b_s[l][:, :t_d], (1, reps)))
    x1_d, route_d, v_d = _post(xd, o_d, ws_d, bt_d, t_d, True, p)

    route = jnp.concatenate([route_p, route_d], axis=0)
    rank, counts = _rank(route)
    expert = route[:, :TOP_K].astype(jnp.int32)
    sizes = counts[0, :N_EXPERTS].astype(jnp.int32)
    padded = (sizes + ROW_BLOCK - 1) // ROW_BLOCK * ROW_BLOCK
    pad_end = jnp.cumsum(padded)
    pad_start = pad_end - padded
    pos = pad_start[expert] + rank[:, :TOP_K].astype(jnp.int32)
    n_rows = rows_p + rows_d
    n_blocks = -(-(n_rows * TOP_K) // ROW_BLOCK) + N_EXPERTS
    block_start = jnp.arange(n_blocks, dtype=jnp.int32) * ROW_BLOCK
    block_expert = jnp.minimum(jnp.searchsorted(pad_end, block_start, side="right"), N_EXPERTS - 1).astype(jnp.int32)
    n_used = (pad_end[-1:] // ROW_BLOCK).astype(jnp.int32)
    pos3 = pos.reshape(n_rows // ROW_TILE, 1, TOP_K * ROW_TILE)
    tiles_p = rows_p // ROW_TILE

    xs = jnp.zeros((n_blocks * ROW_BLOCK, D_MODEL), F32)
    xs = _dispatch(pos3[:tiles_p], x1_p, p["norm2_g"], xs)
    xs = _dispatch(pos3[tiles_p:], x1_d, p["norm2_g"], xs)
    ys = _experts(block_expert, n_used, xs, w_gate[l], w_up[l], w_down[l])
    nf = norm_f_g[None, :]
    y_p = _combine(pos3[:tiles_p], x1_p, route_p, nf, ys)
    y_d = _combine(pos3[tiles_p:], x1_d, route_d, nf, ys)

    n_pp = s_p // PAGE_SIZE
    return (y_p.reshape(b_p, s_p, D_MODEL),
            y_d.reshape(b_d, t_d, D_MODEL),
            ckv_p.reshape(depth, b_p, n_pp, PAGE_SIZE, KV_LORA),
            kr_p.reshape(depth, b_p, n_pp, PAGE_SIZE, ROPE_DIM),
            ckv_d.reshape(depth, b_d, t_d, KV_LORA),
            kr_d.reshape(depth, b_d, t_d, ROPE_DIM),
            v_d.reshape(depth, b_d, t_d, D_MODEL))
```

```python
import functools
import math

import jax
import jax.numpy as jnp
from jax import lax
from jax.experimental import pallas as pl
from jax.experimental.pallas import tpu as pltpu

F32 = jnp.float32
BF16 = jnp.bfloat16

D_MODEL = 1024
PAGE_SIZE = 128
N_HEADS = 8
V_DIM = 128
Q_LORA = 384
KV_LORA = 256
NOPE_DIM = 128
ROPE_DIM = 64
ROPE_HALF = ROPE_DIM // 2
ROPE_THETA = 10000.0
ATTN_SCALE = (NOPE_DIM + ROPE_DIM) ** -0.5
Q_SCALE = ATTN_SCALE * math.log2(math.e)
CHUNK = 128
SGU_GROUPS = 8
SGU_GROUP_WIDTH = 128
N_GROUPS = 8
EXPERTS_PER_GROUP = 8
N_EXPERTS = N_GROUPS * EXPERTS_PER_GROUP
TOP_K = 2
D_EXPERT = 256
ROW_BLOCK = 256
EPS = 1e-6

LANES = 128
ROW_TILE = 256
Q_TOKENS = 128
PAGES_PER_STEP = 16
ATTN_ROW_CHUNK = 256
VMEM_LIMIT = 48 * 1024 * 1024
SMALL_IN = Q_LORA + KV_LORA + 2 * LANES
NEG_INF = float("-inf")


def _rms(x, g):
    return x * lax.rsqrt(jnp.mean(x * x, axis=-1, keepdims=True) + EPS) * g


def _dot(a, b):
    return jnp.dot(a, b, preferred_element_type=F32)


def _dot_nt(a, b):
    return lax.dot_general(a, b, (((1,), (1,)), ((), ())), preferred_element_type=F32)


def _const_spec(shape):
    nd = len(shape)
    return pl.BlockSpec(shape, lambda *_: (0,) * nd)


def _rope_table_kernel(cos_ref, sin_ref, *, seq, past_len, dec_seq):
    n = cos_ref.shape[0]
    row = lax.broadcasted_iota(jnp.int32, (n, LANES), 0)
    lane = lax.broadcasted_iota(jnp.int32, (n, LANES), 1)
    pos = jnp.where(row < seq, row, past_len + (row - seq) % dec_seq)
    j = (lane % ROPE_HALF).astype(F32)
    freq = jnp.power(jnp.full(j.shape, ROPE_THETA, F32), j * (-1.0 / ROPE_HALF))
    ang = pos.astype(F32) * freq
    cos_ref[...] = jnp.cos(ang)
    sin_ref[...] = jnp.sin(ang)


def _rope_tables(seq, past_len, dec_seq):
    n = seq + ROW_TILE
    return pl.pallas_call(
        functools.partial(_rope_table_kernel, seq=seq, past_len=past_len, dec_seq=dec_seq),
        out_shape=(jax.ShapeDtypeStruct((n, LANES), F32), jax.ShapeDtypeStruct((n, LANES), F32)),
        name="rope_table",
    )()


def _proj_kernel(x_ref, cos_ref, sin_ref, n1_ref, win_ref, qn_ref, kvn_ref, wuq_ref, wqr_ref, wuk_ref,
                 ql_ref, qr_ref, ckv_ref, kr_ref, ckvb_ref, krb_ref):
    xn = _rms(x_ref[...], n1_ref[...]).astype(BF16)
    z = _dot(xn, win_ref[...])
    c_q = _rms(z[:, :Q_LORA], qn_ref[...]).astype(BF16)
    c_kv = _rms(z[:, Q_LORA:Q_LORA + KV_LORA], kvn_ref[...])
    cos = cos_ref[...]
    sin = sin_ref[...]
    kr0 = Q_LORA + KV_LORA
    k_rope = z[:, kr0:kr0 + LANES] * cos + z[:, kr0 + LANES:kr0 + 2 * LANES] * sin
    ckv_ref[...] = c_kv
    kr_ref[...] = k_rope[:, :ROPE_DIM]
    ckvb_ref[...] = c_kv.astype(BF16)
    krb_ref[...] = k_rope.astype(BF16)
    q_nope = _dot(c_q, wuq_ref[...]).astype(BF16)
    qr2 = _dot(c_q, wqr_ref[...])
    hw = N_HEADS * LANES
    cos8 = jnp.concatenate([cos] * N_HEADS, axis=1)
    sin8 = jnp.concatenate([sin] * N_HEADS, axis=1)
    q_rope = qr2[:, :hw] * cos8 + qr2[:, hw:] * sin8
    qr_ref[...] = (q_rope * Q_SCALE).astype(BF16)
    for h in range(N_HEADS):
        q_lat = _dot(q_nope[:, h * NOPE_DIM:(h + 1) * NOPE_DIM], wuk_ref[h])
        ql_ref[:, h * KV_LORA:(h + 1) * KV_LORA] = (q_lat * Q_SCALE).astype(BF16)


def _proj(x, cos_t, sin_t, table_block, p):
    rows = x.shape[0]
    nt = rows // ROW_TILE
    row_spec = lambda w: pl.BlockSpec((ROW_TILE, w), lambda i: (i, 0))
    tab_spec = pl.BlockSpec((ROW_TILE, LANES), lambda i: (table_block(i), 0))
    return pl.pallas_call(
        _proj_kernel,
        grid=(nt,),
        in_specs=[row_spec(D_MODEL), tab_spec, tab_spec,
                  _const_spec((1, D_MODEL)), _const_spec((D_MODEL, SMALL_IN)),
                  _const_spec((1, Q_LORA)), _const_spec((1, KV_LORA)),
                  _const_spec((Q_LORA, N_HEADS * NOPE_DIM)), _const_spec((Q_LORA, 2 * N_HEADS * LANES)),
                  _const_spec((N_HEADS, NOPE_DIM, KV_LORA))],
        out_specs=[row_spec(N_HEADS * KV_LORA), row_spec(N_HEADS * LANES), row_spec(KV_LORA),
                   row_spec(ROPE_DIM), row_spec(KV_LORA), row_spec(LANES)],
        out_shape=[jax.ShapeDtypeStruct((rows, N_HEADS * KV_LORA), BF16),
                   jax.ShapeDtypeStruct((rows, N_HEADS * LANES), BF16),
                   jax.ShapeDtypeStruct((rows, KV_LORA), F32),
                   jax.ShapeDtypeStruct((rows, ROPE_DIM), F32),
                   jax.ShapeDtypeStruct((rows, KV_LORA), BF16),
                   jax.ShapeDtypeStruct((rows, LANES), BF16)],
        compiler_params=pltpu.CompilerParams(dimension_semantics=("parallel",), vmem_limit_bytes=VMEM_LIMIT),
        name="in_proj",
    )(x, cos_t, sin_t, p["norm1_g"], p["w_in_small"], p["q_norm_g"], p["kv_norm_g"], p["w_uq"], p["w_qr2"],
      p["w_uk_t"])


def _prompt_attn_kernel(ql_ref, qr_ref, ckv_ref, kr_ref, o_ref, q_s, qr_s, sa_buf, sb_buf, acc_ref, m_ref,
                        l_ref):
    qi = pl.program_id(1)
    for h in range(N_HEADS):
        q_s[h * Q_TOKENS:(h + 1) * Q_TOKENS, :] = ql_ref[:, h * KV_LORA:(h + 1) * KV_LORA]
        qr_s[h * Q_TOKENS:(h + 1) * Q_TOKENS, :] = qr_ref[:, h * LANES:(h + 1) * LANES]
    m_ref[...] = jnp.full(m_ref.shape, NEG_INF, F32)
    l_ref[...] = jnp.zeros(l_ref.shape, F32)
    acc_ref[...] = jnp.zeros(acc_ref.shape, F32)

    def keys(j):
        off = pl.multiple_of(j * Q_TOKENS, Q_TOKENS)
        return ckv_ref[pl.ds(off, Q_TOKENS), :], kr_ref[pl.ds(off, Q_TOKENS), :]

    def scores(j, s_buf):
        k, kr = keys(j)
        s_buf[...] = _dot_nt(q_s[...], k) + _dot_nt(qr_s[...], kr)

    def consume(j, s_buf, masked):
        k, _ = keys(j)
        for c in range(N_HEADS * Q_TOKENS // ATTN_ROW_CHUNK):
            rows = pl.ds(c * ATTN_ROW_CHUNK, ATTN_ROW_CHUNK)
            s = s_buf[rows, :]
            if masked:
                row = lax.broadcasted_iota(jnp.int32, s.shape, 0)
                col = lax.broadcasted_iota(jnp.int32, s.shape, 1)
                s = jnp.where(col <= row % Q_TOKENS, s, NEG_INF)
            m_prev = m_ref[rows, :]
            m_new = jnp.maximum(m_prev, jnp.max(s, axis=-1, keepdims=True))
            alpha = jnp.exp2(m_prev - m_new)
            p = jnp.exp2(s - m_new)
            l_ref[rows, :] = alpha * l_ref[rows, :] + jnp.sum(p, axis=-1, keepdims=True)
            acc_ref[rows, :] = acc_ref[rows, :] * jnp.concatenate([alpha, alpha], axis=1) + _dot(p.astype(BF16), k)
            m_ref[rows, :] = m_new

    scores(0, sa_buf)

    def body(jj, carry):
        j = 2 * jj
        scores(j + 1, sb_buf)
        consume(j, sa_buf, False)
        scores(j + 2, sa_buf)
        consume(j + 1, sb_buf, False)
        return carry

    lax.fori_loop(0, qi // 2, body, 0)

    @pl.when(qi % 2 == 1)
    def _():
        scores(qi, sb_buf)
        consume(qi - 1, sa_buf, False)
        consume(qi, sb_buf, True)

    @pl.when(qi % 2 == 0)
    def _():
        consume(qi, sa_buf, True)

    inv = 1.0 / l_ref[...]
    o = acc_ref[...] * jnp.concatenate([inv, inv], axis=1)
    for h in range(N_HEADS):
        o_ref[:, h * KV_LORA:(h + 1) * KV_LORA] = o[h * Q_TOKENS:(h + 1) * Q_TOKENS, :].astype(BF16)


def _prompt_attn(q_lat, q_rope, ckv_b, kr_b, batch, seq):
    nq = seq // Q_TOKENS
    hq = N_HEADS * Q_TOKENS
    return pl.pallas_call(
        _prompt_attn_kernel,
        grid=(batch, nq),
        in_specs=[pl.BlockSpec((Q_TOKENS, N_HEADS * KV_LORA), lambda b, i: (b * nq + i, 0)),
                  pl.BlockSpec((Q_TOKENS, N_HEADS * LANES), lambda b, i: (b * nq + i, 0)),
                  pl.BlockSpec((seq, KV_LORA), lambda b, i: (b, 0)),
                  pl.BlockSpec((seq, LANES), lambda b, i: (b, 0))],
        out_specs=pl.BlockSpec((Q_TOKENS, N_HEADS * KV_LORA), lambda b, i: (b * nq + i, 0)),
        out_shape=jax.ShapeDtypeStruct((batch * seq, N_HEADS * KV_LORA), BF16),
        scratch_shapes=[pltpu.VMEM((hq, KV_LORA), BF16), pltpu.VMEM((hq, LANES), BF16),
                        pltpu.VMEM((hq, Q_TOKENS), F32), pltpu.VMEM((hq, Q_TOKENS), F32),
                        pltpu.VMEM((hq, KV_LORA), F32), pltpu.VMEM((hq, LANES), F32),
                        pltpu.VMEM((hq, LANES), F32)],
        compiler_params=pltpu.CompilerParams(dimension_semantics=("parallel", "arbitrary"),
                                             vmem_limit_bytes=VMEM_LIMIT),
        name="prompt_attn",
    )(q_lat, q_rope, ckv_b, kr_b)


def _sample_attn_kernel(pt_ref, q_ref, qr_ref, ckn_ref, krn_ref, ck_hbm, kr_hbm, o_ref,
                        ck_buf, kr_buf, sems, acc_ref, m_ref, l_ref, *, batch, n_pages, dec_seq):
    npg = PAGES_PER_STEP
    n_chunks = n_pages // npg
    b = pl.program_id(0)
    total = batch * n_chunks
    q = q_ref[...]
    qr = qr_ref[...]

    def start_chunk(t, slot):
        for i in range(npg):
            page = pt_ref[t * npg + i]
            pltpu.make_async_copy(ck_hbm.at[page], ck_buf.at[slot, i], sems.at[0, slot]).start()
            pltpu.make_async_copy(kr_hbm.at[page], kr_buf.at[slot, i], sems.at[1, slot]).start()

    def wait_chunk(slot):
        pltpu.make_async_copy(ck_buf.at[slot], ck_buf.at[slot], sems.at[0, slot]).wait()
        pltpu.make_async_copy(kr_buf.at[slot], kr_buf.at[slot], sems.at[1, slot]).wait()

    @pl.when(b == 0)
    def _():
        start_chunk(0, 0)

    kn = ckn_ref[...]
    s = _dot_nt(q, kn) + _dot_nt(qr, krn_ref[...])
    row = lax.broadcasted_iota(jnp.int32, s.shape, 0)
    col = lax.broadcasted_iota(jnp.int32, s.shape, 1)
    s = jnp.where(col <= row % dec_seq, s, NEG_INF)
    m0 = jnp.max(s, axis=-1, keepdims=True)
    p = jnp.exp2(s - m0)
    m_ref[...] = jnp.broadcast_to(m0, m_ref.shape)
    l_ref[...] = jnp.broadcast_to(jnp.sum(p, axis=-1, keepdims=True), l_ref.shape)
    acc_ref[...] = _dot(p.astype(BF16), kn)

    def body(c, carry):
        slot = c % 2
        t = b * n_chunks + c

        @pl.when(t + 1 < total)
        def _():
            start_chunk(t + 1, 1 - slot)

        wait_chunk(slot)
        ks = [ck_buf[slot, i].astype(BF16) for i in range(npg)]
        ss = [_dot_nt(q, ks[i]) + _dot(qr, kr_buf[slot, i].astype(BF16)) for i in range(npg)]
        m_prev = m_ref[...]
        m_cur = jnp.max(functools.reduce(jnp.maximum, ss), axis=-1, keepdims=True)
        m_new = jnp.maximum(m_prev, m_cur)
        alpha = jnp.exp2(m_prev - m_new)
        ps = [jnp.exp2(s - m_new) for s in ss]
        l_ref[...] = alpha * l_ref[...] + jnp.sum(functools.reduce(jnp.add, ps), axis=-1, keepdims=True)
        pv = functools.reduce(jnp.add, [_dot(p.astype(BF16), k) for p, k in zip(ps, ks)])
        acc_ref[...] = acc_ref[...] * jnp.concatenate([alpha, alpha], axis=1) + pv
        m_ref[...] = m_new
        return carry

    lax.fori_loop(0, n_chunks, body, 0)
    inv = 1.0 / l_ref[...]
    o_ref[...] = (acc_ref[...] * jnp.concatenate([inv, inv], axis=1)).astype(BF16)


def _sample_attn(page_table, q, qr, ck_new, kr_new, cache_ckv, cache_krope_t, dec_seq):
    batch, n_pages = page_table.shape
    npg = PAGES_PER_STEP
    assert n_pages % (2 * npg) == 0
    rows = N_HEADS * dec_seq
    per_b = lambda w, r: pl.BlockSpec((None, r, w), lambda b, pt: (b, 0, 0))
    grid_spec = pltpu.PrefetchScalarGridSpec(
        num_scalar_prefetch=1,
        grid=(batch,),
        in_specs=[per_b(KV_LORA, rows), per_b(ROPE_DIM, rows), per_b(KV_LORA, PAGE_SIZE), per_b(ROPE_DIM, PAGE_SIZE),
                  pl.BlockSpec(memory_space=pl.ANY), pl.BlockSpec(memory_space=pl.ANY)],
        out_specs=per_b(KV_LORA, rows),
        scratch_shapes=[pltpu.VMEM((2, npg, PAGE_SIZE, KV_LORA), F32), pltpu.VMEM((2, npg, ROPE_DIM, PAGE_SIZE), F32),
                        pltpu.SemaphoreType.DMA((2, 2)),
                        pltpu.VMEM((rows, KV_LORA), F32), pltpu.VMEM((rows, LANES), F32),
                        pltpu.VMEM((rows, LANES), F32)],
    )
    return pl.pallas_call(
        functools.partial(_sample_attn_kernel, batch=batch, n_pages=n_pages, dec_seq=dec_seq),
        grid_spec=grid_spec,
        out_shape=jax.ShapeDtypeStruct((batch, rows, KV_LORA), BF16),
        compiler_params=pltpu.CompilerParams(dimension_semantics=("arbitrary",), vmem_limit_bytes=VMEM_LIMIT),
        name="sample_attn",
    )(page_table.reshape(-1), q, qr, ck_new, kr_new, cache_ckv, cache_krope_t)


def _route(logits):
    lane = lax.broadcasted_iota(jnp.int32, logits.shape, 1)
    big = jnp.int32(LANES)
    gl = jnp.where(lane < N_GROUPS, logits, NEG_INF)
    gmax = jnp.max(gl, axis=-1, keepdims=True)
    gsum = jnp.sum(jnp.exp(gl - gmax), axis=-1, keepdims=True)
    g_w = 1.0 / gsum
    g_idx = jnp.min(jnp.where(gl == gmax, lane, big), axis=-1, keepdims=True)
    lo = N_GROUPS + g_idx * EXPERTS_PER_GROUP
    el = jnp.where((lane >= lo) & (lane < lo + EXPERTS_PER_GROUP), logits, NEG_INF)
    m1 = jnp.max(el, axis=-1, keepdims=True)
    i1 = jnp.min(jnp.where(el == m1, lane, big), axis=-1, keepdims=True)
    el2 = jnp.where(lane == i1, NEG_INF, el)
    m2 = jnp.max(el2, axis=-1, keepdims=True)
    i2 = jnp.min(jnp.where(el2 == m2, lane, big), axis=-1, keepdims=True)
    esum = jnp.sum(jnp.exp(el - m1), axis=-1, keepdims=True)
    p1 = 1.0 / esum
    p2 = jnp.exp(m2 - m1) / esum
    tot = p1 + p2
    return i1 - N_GROUPS, i2 - N_GROUPS, g_w * (p1 / tot), g_w * (p2 / tot)


def _post_kernel(x_ref, o_ref, n1_ref, win_ref, sg_ref, wuv_ref, ws_ref, bt_ref, wo_ref, n2_ref,
                 wrh_ref, wrl_ref, br_ref, x1_ref, route_ref, *v_out, chunk_rows):
    x = x_ref[...]
    xn = _rms(x, n1_ref[...]).astype(BF16)
    z = _dot(xn, win_ref[...])
    u = jax.nn.gelu(z[:, :D_MODEL])
    v = _rms(jax.nn.gelu(z[:, D_MODEL:2 * D_MODEL]), sg_ref[...])
    if v_out:
        v_out[0][...] = v
    ga = jax.nn.sigmoid(z[:, 2 * D_MODEL:3 * D_MODEL])
    gb = jax.nn.sigmoid(z[:, 3 * D_MODEL:])
    attn = jnp.concatenate(
        [_dot(o_ref[:, h * KV_LORA:(h + 1) * KV_LORA], wuv_ref[h]) for h in range(N_HEADS)], axis=1)
    row = lax.broadcasted_iota(jnp.int32, (CHUNK, CHUNK), 0)
    col = lax.broadcasted_iota(jnp.int32, (CHUNK, CHUNK), 1)
    keep = (col <= row) & (row // chunk_rows == col // chunk_rows)
    vb = v.astype(BF16)
    bt = bt_ref[...]
    mixed_rows = []
    for c in range(ROW_TILE // CHUNK):
        groups = []
        for g in range(SGU_GROUPS):
            ws = jnp.where(keep, ws_ref[g], 0.0).astype(BF16)
            vg = vb[c * CHUNK:(c + 1) * CHUNK, g * SGU_GROUP_WIDTH:(g + 1) * SGU_GROUP_WIDTH]
            groups.append(_dot(ws, vg) + bt[:, g:g + 1])
        mixed_rows.append(jnp.concatenate(groups, axis=1))
    mixed = jnp.concatenate(mixed_rows, axis=0)
    merged = ga * attn + gb * (u * mixed)
    x1 = x + _dot(merged.astype(BF16), wo_ref[...])
    x1_ref[...] = x1
    xt = _rms(x1, n2_ref[...])
    xt_hi = xt.astype(BF16)
    xt_lo = (xt - xt_hi.astype(F32)).astype(BF16)
    logits = (_dot(xt_hi, wrh_ref[...]) + _dot(xt_lo, wrh_ref[...]) + _dot(xt_hi, wrl_ref[...])) + br_ref[...]
    e0, e1, w0, w1 = _route(logits)
    lane = lax.broadcasted_iota(jnp.int32, logits.shape, 1)
    route_ref[...] = jnp.where(lane == 0, e0.astype(F32),
                               jnp.where(lane == 1, e1.astype(F32),
                                         jnp.where(lane == 2, w0, jnp.where(lane == 3, w1, 0.0))))


def _post(x, o_lat, ws_eff, bt_eff, chunk_rows, emit_v, p):
    rows = x.shape[0]
    nt = rows // ROW_TILE
    row_spec = lambda w: pl.BlockSpec((ROW_TILE, w), lambda i: (i, 0))
    out_specs = [row_spec(D_MODEL), row_spec(LANES)]
    out_shape = [jax.ShapeDtypeStruct((rows, D_MODEL), F32), jax.ShapeDtypeStruct((rows, LANES), F32)]
    if emit_v:
        out_specs.append(row_spec(D_MODEL))
        out_shape.append(jax.ShapeDtypeStruct((rows, D_MODEL), F32))
    return pl.pallas_call(
        functools.partial(_post_kernel, chunk_rows=chunk_rows),
        grid=(nt,),
        in_specs=[row_spec(D_MODEL), row_spec(N_HEADS * KV_LORA),
                  _const_spec((1, D_MODEL)), _const_spec((D_MODEL, 4 * D_MODEL)), _const_spec((1, D_MODEL)),
                  _const_spec((N_HEADS, KV_LORA, V_DIM)), _const_spec((SGU_GROUPS, CHUNK, CHUNK)),
                  _const_spec((CHUNK, SGU_GROUPS)), _const_spec((D_MODEL, D_MODEL)), _const_spec((1, D_MODEL)),
                  _const_spec((D_MODEL, LANES)), _const_spec((D_MODEL, LANES)), _const_spec((1, LANES))],
        out_specs=out_specs,
        out_shape=out_shape,
        compiler_params=pltpu.CompilerParams(dimension_semantics=("parallel",), vmem_limit_bytes=VMEM_LIMIT),
        name="merge_route",
    )(x, o_lat, p["norm1_g"], p["w_in_big"], p["sgu_norm_g"], p["w_uv_h"], ws_eff, bt_eff, p["w_o"],
      p["norm2_g"], p["w_r_hi"], p["w_r_lo"], p["b_r"])


def _rank_kernel(route_ref, rank_ref, counts_ref, carry_ref):
    @pl.when(pl.program_id(0) == 0)
    def _():
        carry_ref[...] = jnp.zeros(carry_ref.shape, F32)

    r = route_ref[...]
    lane = lax.broadcasted_iota(jnp.int32, r.shape, 1)
    oh0 = jnp.where(lane == r[:, 0:1].astype(jnp.int32), 1.0, 0.0)
    oh1 = jnp.where(lane == r[:, 1:2].astype(jnp.int32), 1.0, 0.0)
    both = oh0 + oh1
    row = lax.broadcasted_iota(jnp.int32, (ROW_TILE, ROW_TILE), 0)
    col = lax.broadcasted_iota(jnp.int32, (ROW_TILE, ROW_TILE), 1)
    before = jnp.where(col < row, 1.0, 0.0).astype(BF16)
    base = carry_ref[...] + _dot(before, both.astype(BF16))
    rank0 = jnp.sum(oh0 * base, axis=-1, keepdims=True)
    rank1 = jnp.sum(oh1 * (base + oh0), axis=-1, keepdims=True)
    rank_ref[...] = jnp.where(lane == 0, rank0, jnp.where(lane == 1, rank1, 0.0))
    carry_ref[...] = carry_ref[...] + jnp.sum(both, axis=0, keepdims=True)
    counts_ref[...] = carry_ref[...]


def _rank(route):
    rows = route.shape[0]
    return pl.pallas_call(
        _rank_kernel,
        grid=(rows // ROW_TILE,),
        in_specs=[pl.BlockSpec((ROW_TILE, LANES), lambda i: (i, 0))],
        out_specs=[pl.BlockSpec((ROW_TILE, LANES), lambda i: (i, 0)), _const_spec((1, LANES))],
        out_shape=[jax.ShapeDtypeStruct((rows, LANES), F32), jax.ShapeDtypeStruct((1, LANES), F32)],
        scratch_shapes=[pltpu.VMEM((1, LANES), F32)],
        compiler_params=pltpu.CompilerParams(dimension_semantics=("arbitrary",)),
        name="expert_rank",
    )(route)


def _row_copy(src, src_row, dst, dst_row, sem):
    return pltpu.make_async_copy(src.at[pl.ds(src_row, 1)], dst.at[pl.ds(dst_row, 1)], sem)


def _dispatch_kernel(pos_ref, x1_ref, n2_ref, xs_in_ref, xs_ref, xt_ref, sems, *, n_tiles):
    del xs_in_ref
    i = pl.program_id(0)
    slot = i % 2
    xt_ref[slot] = _rms(x1_ref[...], n2_ref[...])

    def issue(r, carry):
        for k in range(TOP_K):
            _row_copy(xt_ref.at[slot], r, xs_ref, pos_ref[0, 0, TOP_K * r + k], sems.at[slot]).start()
        return carry

    lax.fori_loop(0, ROW_TILE, issue, 0, unroll=8)

    def drain(s):
        for k in range(TOP_K):
            pltpu.make_async_copy(xt_ref.at[s], xt_ref.at[s], sems.at[s]).wait()

    @pl.when(i > 0)
    def _():
        drain(1 - slot)

    @pl.when(i == n_tiles - 1)
    def _():
        drain(slot)


def _dispatch(pos3, x1, norm2_g, xs):
    rows = x1.shape[0]
    return pl.pallas_call(
        functools.partial(_dispatch_kernel, n_tiles=rows // ROW_TILE),
        grid=(rows // ROW_TILE,),
        in_specs=[pl.BlockSpec((1, 1, TOP_K * ROW_TILE), lambda i: (i, 0, 0), memory_space=pltpu.SMEM),
                  pl.BlockSpec((ROW_TILE, D_MODEL), lambda i: (i, 0)),
                  _const_spec((1, D_MODEL)),
                  pl.BlockSpec(memory_space=pl.ANY)],
        out_specs=pl.BlockSpec(memory_space=pl.ANY),
        out_shape=jax.ShapeDtypeStruct(xs.shape, xs.dtype),
        scratch_shapes=[pltpu.VMEM((2, ROW_TILE, D_MODEL), F32), pltpu.SemaphoreType.DMA((2,))],
        input_output_aliases={3: 0},
        compiler_params=pltpu.CompilerParams(dimension_semantics=("arbitrary",), has_side_effects=True),
        name="dispatch",
    )(pos3, x1, norm2_g, xs)


def _expert_kernel(be_ref, nb_ref, xs_ref, wg_ref, wu_ref, wd_ref, ys_ref, wg_s, wu_s, wd_s):
    i = pl.program_id(0)
    prev = be_ref[jnp.maximum(i - 1, 0)]

    @pl.when((i == 0) | (be_ref[i] != prev))
    def _():
        wg_s[...] = wg_ref[...].astype(BF16)
        wu_s[...] = wu_ref[...].astype(BF16)
        wd_s[...] = wd_ref[...].astype(BF16)

    @pl.when(i < nb_ref[0])
    def _():
        xb = xs_ref[...].astype(BF16)
        h = jax.nn.silu(_dot(xb, wg_s[...])) * _dot(xb, wu_s[...])
        ys_ref[...] = _dot(h.astype(BF16), wd_s[...])

    @pl.when(i >= nb_ref[0])
    def _():
        ys_ref[...] = jnp.zeros(ys_ref.shape, F32)


def _experts(block_expert, n_used, xs, w_gate, w_up, w_down):
    n_blocks = xs.shape[0] // ROW_BLOCK
    grid_spec = pltpu.PrefetchScalarGridSpec(
        num_scalar_prefetch=2,
        grid=(n_blocks,),
        in_specs=[pl.BlockSpec((ROW_BLOCK, D_MODEL), lambda i, be, nb: (i, 0)),
                  pl.BlockSpec((None, D_MODEL, D_EXPERT), lambda i, be, nb: (be[i], 0, 0)),
                  pl.BlockSpec((None, D_MODEL, D_EXPERT), lambda i, be, nb: (be[i], 0, 0)),
                  pl.BlockSpec((None, D_EXPERT, D_MODEL), lambda i, be, nb: (be[i], 0, 0))],
        out_specs=pl.BlockSpec((ROW_BLOCK, D_MODEL), lambda i, be, nb: (i, 0)),
        scratch_shapes=[pltpu.VMEM((D_MODEL, D_EXPERT), BF16), pltpu.VMEM((D_MODEL, D_EXPERT), BF16),
                        pltpu.VMEM((D_EXPERT, D_MODEL), BF16)],
    )
    return pl.pallas_call(
        _expert_kernel,
        grid_spec=grid_spec,
        out_shape=jax.ShapeDtypeStruct(xs.shape, F32),
        compiler_params=pltpu.CompilerParams(dimension_semantics=("arbitrary",), vmem_limit_bytes=VMEM_LIMIT),
        name="expert_ffn",
    )(block_expert, n_used, xs, w_gate, w_up, w_down)


def _combine_kernel(pos_ref, pos_next_ref, x1_ref, route_ref, nf_ref, ys_ref, y_ref, buf_ref, sems, *, n_tiles):
    i = pl.program_id(0)
    slot = i % 2

    def gather(p_ref, s):
        def issue(r, carry):
            for k in range(TOP_K):
                _row_copy(ys_ref, p_ref[0, 0, TOP_K * r + k], buf_ref.at[s, k], r, sems.at[s]).start()
            return carry

        lax.fori_loop(0, ROW_TILE, issue, 0, unroll=8)

    @pl.when(i == 0)
    def _():
        gather(pos_ref, 0)

    @pl.when(i + 1 < n_tiles)
    def _():
        gather(pos_next_ref, 1 - slot)

    pltpu.make_async_copy(buf_ref.at[slot], buf_ref.at[slot], sems.at[slot]).wait()
    route = route_ref[...]
    y = x1_ref[...] + buf_ref[slot, 0] * route[:, 2:3] + buf_ref[slot, 1] * route[:, 3:4]
    y_ref[...] = _rms(y, nf_ref[...])


def _combine(pos3, x1, route, norm_f_g, ys):
    rows = x1.shape[0]
    nt = rows // ROW_TILE
    return pl.pallas_call(
        functools.partial(_combine_kernel, n_tiles=nt),
        grid=(nt,),
        in_specs=[pl.BlockSpec((1, 1, TOP_K * ROW_TILE), lambda i: (i, 0, 0), memory_space=pltpu.SMEM),
                  pl.BlockSpec((1, 1, TOP_K * ROW_TILE), lambda i: (jnp.minimum(i + 1, nt - 1), 0, 0),
                               memory_space=pltpu.SMEM),
                  pl.BlockSpec((ROW_TILE, D_MODEL), lambda i: (i, 0)),
                  pl.BlockSpec((ROW_TILE, LANES), lambda i: (i, 0)),
                  _const_spec((1, D_MODEL)),
                  pl.BlockSpec(memory_space=pl.ANY)],
        out_specs=pl.BlockSpec((ROW_TILE, D_MODEL), lambda i: (i, 0)),
        out_shape=jax.ShapeDtypeStruct((rows, D_MODEL), F32),
        scratch_shapes=[pltpu.VMEM((2, TOP_K, ROW_TILE, D_MODEL), F32), pltpu.SemaphoreType.DMA((2,))],
        compiler_params=pltpu.CompilerParams(dimension_semantics=("arbitrary",)),
        name="combine",
    )(pos3, pos3, x1, route, norm_f_g, ys)


def _prepare_params(norm1_g, w_in, q_norm_g, kv_norm_g, w_uq, w_qr, w_uk, w_uv, sgu_norm_g, w_o, norm2_g,
                    w_rg, b_rg, w_re, b_re):
    def rot(w):
        return jnp.concatenate([-w[..., ROPE_HALF:], w[..., :ROPE_HALF]], axis=-1)

    q0, kv0, kr0 = 0, Q_LORA, Q_LORA + KV_LORA
    big0 = kr0 + ROPE_DIM
    w_kr = w_in[:, kr0:big0]
    w_in_small = jnp.concatenate([w_in[:, q0:kr0], w_kr, w_kr, rot(w_kr), rot(w_kr)], axis=1).astype(BF16)
    zeros = jnp.zeros((Q_LORA, N_HEADS, LANES - ROPE_DIM), w_qr.dtype)
    pad = lambda w: jnp.concatenate([w, zeros], axis=-1).reshape(Q_LORA, N_HEADS * LANES)
    w_qr2 = jnp.concatenate([pad(w_qr), pad(rot(w_qr))], axis=1).astype(BF16)
    w_r = jnp.concatenate([w_rg, w_re, jnp.zeros((D_MODEL, LANES - N_GROUPS - N_EXPERTS), F32)], axis=1)
    w_r_hi = w_r.astype(BF16)
    b_r = jnp.concatenate([b_rg, b_re, jnp.zeros((LANES - N_GROUPS - N_EXPERTS,), F32)])[None, :]
    return dict(
        norm1_g=norm1_g[None, :], w_in_small=w_in_small, w_in_big=w_in[:, big0:].astype(BF16),
        q_norm_g=q_norm_g[None, :], kv_norm_g=kv_norm_g[None, :],
        w_uq=w_uq.reshape(Q_LORA, N_HEADS * NOPE_DIM).astype(BF16), w_qr2=w_qr2,
        w_uk_t=jnp.transpose(w_uk, (1, 2, 0)).astype(BF16), w_uv_h=jnp.transpose(w_uv, (1, 0, 2)).astype(BF16),
        sgu_norm_g=sgu_norm_g[None, :], w_o=w_o.astype(BF16), norm2_g=norm2_g[None, :],
        w_r_hi=w_r_hi, w_r_lo=(w_r - w_r_hi.astype(F32)).astype(BF16), b_r=b_r)


def _moe_and_final_norm(x1s, routes, norm2_g, w_gate, w_up, w_down, norm_f_g):
    route = jnp.concatenate(routes, axis=0)
    rank, counts = _rank(route)
    expert = route[:, :TOP_K].astype(jnp.int32)
    sizes = counts[0, :N_EXPERTS].astype(jnp.int32)
    padded = (sizes + ROW_BLOCK - 1) // ROW_BLOCK * ROW_BLOCK
    pad_end = jnp.cumsum(padded)
    pad_start = pad_end - padded
    start_of = jnp.sum(jnp.where(expert[..., None] == jnp.arange(N_EXPERTS, dtype=jnp.int32), pad_start, 0), axis=-1)
    pos = start_of + rank[:, :TOP_K].astype(jnp.int32)
    n_rows = route.shape[0]
    n_blocks = -(-(n_rows * TOP_K) // ROW_BLOCK) + N_EXPERTS
    block_start = jnp.arange(n_blocks, dtype=jnp.int32) * ROW_BLOCK
    block_expert = jnp.minimum(jnp.sum(pad_end[None, :] <= block_start[:, None], axis=1), N_EXPERTS - 1).astype(jnp.int32)
    n_used = (pad_end[-1:] // ROW_BLOCK).astype(jnp.int32)
    pos3 = pos.reshape(n_rows // ROW_TILE, 1, TOP_K * ROW_TILE)
    tile_ends = [0]
    for x1 in x1s:
        tile_ends.append(tile_ends[-1] + x1.shape[0] // ROW_TILE)
    pos3s = [pos3[a:b] for a, b in zip(tile_ends[:-1], tile_ends[1:])]

    xs = jnp.zeros((n_blocks * ROW_BLOCK, D_MODEL), F32)
    for x1, p3 in zip(x1s, pos3s):
        xs = _dispatch(p3, x1, norm2_g, xs)
    ys = _experts(block_expert, n_used, xs, w_gate, w_up, w_down)
    return [_combine(p3, x1, r, norm_f_g, ys) for x1, r, p3 in zip(x1s, routes, pos3s)]


def kernel(x_prompt, x_sample, cache_ckv, cache_krope, page_table, norm1_g, w_in, q_norm_g, kv_norm_g, w_uq, w_qr,
           w_uk, w_uv, sgu_norm_g, w_s, b_s, w_o, norm2_g, w_rg, b_rg, w_re, b_re, w_gate, w_up, w_down, norm_f_g):
    b_p, s_p, _ = x_prompt.shape
    b_d, t_d, _ = x_sample.shape
    depth = norm1_g.shape[0]
    past_len = page_table.shape[1] * PAGE_SIZE
    rows_p, rows_d = b_p * s_p, b_d * t_d
    assert depth == 1 and s_p % ROW_TILE == 0 and rows_d % ROW_TILE == 0 and CHUNK % t_d == 0
    cos_t, sin_t = _rope_tables(s_p, past_len, t_d)
    tiles_per_seq = s_p // ROW_TILE

    xp = x_prompt.reshape(rows_p, D_MODEL)
    xd = x_sample.reshape(rows_d, D_MODEL)
    l = 0
    p = _prepare_params(norm1_g[l], w_in[l], q_norm_g[l], kv_norm_g[l], w_uq[l], w_qr[l], w_uk[l], w_uv[l],
                        sgu_norm_g[l], w_o[l], norm2_g[l], w_rg[l], b_rg[l], w_re[l], b_re[l])

    ql_p, qr_p, ckv_p, kr_p, ckvb_p, krb_p = _proj(xp, cos_t, sin_t, lambda i: i % tiles_per_seq, p)
    o_p = _prompt_attn(ql_p, qr_p, ckvb_p, krb_p, b_p, s_p)
    bt_p = jnp.transpose(b_s[l])
    x1_p, route_p = _post(xp, o_p, w_s[l], bt_p, CHUNK, False, p)

    ql_d, qr_d, ckv_d, kr_d, ckvb_d, krb_d = _proj(xd, cos_t, sin_t, lambda i: tiles_per_seq, p)
    hr = N_HEADS * t_d
    q_d = ql_d.reshape(b_d, t_d, N_HEADS, KV_LORA).transpose(0, 2, 1, 3).reshape(b_d, hr, KV_LORA)
    qrope_d = qr_d.reshape(b_d, t_d, N_HEADS, LANES)[..., :ROPE_DIM].transpose(0, 2, 1, 3).reshape(b_d, hr, ROPE_DIM)
    pad_rows = lambda a: jnp.pad(a.reshape(b_d, t_d, -1), ((0, 0), (0, PAGE_SIZE - t_d), (0, 0)))
    o_d = _sample_attn(page_table, q_d, qrope_d, pad_rows(ckvb_d), pad_rows(krb_d[:, :ROPE_DIM]),
                       cache_ckv[l], jnp.swapaxes(cache_krope[l], 1, 2), t_d)
    o_d = o_d.reshape(b_d, N_HEADS, t_d, KV_LORA).transpose(0, 2, 1, 3).reshape(rows_d, N_HEADS * KV_LORA)
    reps = CHUNK // t_d
    ws_d = jnp.tile(w_s[l][:, :t_d, :t_d], (1, reps, reps))
    bt_d = jnp.transpose(jnp.tile(b_s[l][:, :t_d], (1, reps)))
    x1_d, route_d, v_d = _post(xd, o_d, ws_d, bt_d, t_d, True, p)

    y_p, y_d = _moe_and_final_norm([x1_p, x1_d], [route_p, route_d], p["norm2_g"], w_gate[l], w_up[l], w_down[l],
                                   norm_f_g[None, :])
    n_pp = s_p // PAGE_SIZE
    return (y_p.reshape(b_p, s_p, D_MODEL),
            y_d.reshape(b_d, t_d, D_MODEL),
            ckv_p.reshape(depth, b_p, n_pp, PAGE_SIZE, KV_LORA),
            kr_p.reshape(depth, b_p, n_pp, PAGE_SIZE, ROPE_DIM),
            ckv_d.reshape(depth, b_d, t_d, KV_LORA),
            kr_d.reshape(depth, b_d, t_d, ROPE_DIM),
            v_d.reshape(depth, b_d, t_d, D_MODEL))
```

```python
import functools
import math

import jax
import jax.numpy as jnp
from jax import lax
from jax.experimental import pallas as pl
from jax.experimental.pallas import tpu as pltpu

F32 = jnp.float32
BF16 = jnp.bfloat16

D_MODEL = 1024
PAGE_SIZE = 128
N_HEADS = 8
V_DIM = 128
Q_LORA = 384
KV_LORA = 256
NOPE_DIM = 128
ROPE_DIM = 64
ROPE_HALF = ROPE_DIM // 2
ROPE_THETA = 10000.0
ATTN_SCALE = (NOPE_DIM + ROPE_DIM) ** -0.5
Q_SCALE = ATTN_SCALE * math.log2(math.e)
CHUNK = 128
SGU_GROUPS = 8
SGU_GROUP_WIDTH = 128
N_GROUPS = 8
EXPERTS_PER_GROUP = 8
N_EXPERTS = N_GROUPS * EXPERTS_PER_GROUP
TOP_K = 2
D_EXPERT = 256
ROW_BLOCK = 256
EPS = 1e-6

LANES = 128
ROW_TILE = 256
Q_TOKENS = 128
KEY_BLOCK = 128
PAGES_PER_STEP = 16
PAGE_SLOTS = 4
ATTN_ROW_CHUNK = 256
VMEM_LIMIT = 48 * 1024 * 1024
SMALL_IN = Q_LORA + KV_LORA + 2 * LANES
NEG_INF = float("-inf")


def _rms(x, g):
    return x * lax.rsqrt(jnp.mean(x * x, axis=-1, keepdims=True) + EPS) * g


def _dot(a, b):
    return jnp.dot(a, b, preferred_element_type=F32)


def _dot_nt(a, b):
    return lax.dot_general(a, b, (((1,), (1,)), ((), ())), preferred_element_type=F32)


def _const_spec(shape):
    nd = len(shape)
    return pl.BlockSpec(shape, lambda *_: (0,) * nd)


def _rope_table_kernel(cos_ref, sin_ref, *, seq, past_len, dec_seq):
    n = cos_ref.shape[0]
    row = lax.broadcasted_iota(jnp.int32, (n, LANES), 0)
    lane = lax.broadcasted_iota(jnp.int32, (n, LANES), 1)
    pos = jnp.where(row < seq, row, past_len + (row - seq) % dec_seq)
    j = (lane % ROPE_HALF).astype(F32)
    freq = jnp.power(jnp.full(j.shape, ROPE_THETA, F32), j * (-1.0 / ROPE_HALF))
    ang = pos.astype(F32) * freq
    cos_ref[...] = jnp.cos(ang)
    sin_ref[...] = jnp.sin(ang)


def _rope_tables(seq, past_len, dec_seq):
    n = seq + ROW_TILE
    return pl.pallas_call(
        functools.partial(_rope_table_kernel, seq=seq, past_len=past_len, dec_seq=dec_seq),
        out_shape=(jax.ShapeDtypeStruct((n, LANES), F32), jax.ShapeDtypeStruct((n, LANES), F32)),
        name="rope_table",
    )()


def _proj_kernel(x_ref, cos_ref, sin_ref, n1_ref, win_ref, qn_ref, kvn_ref, wuq_ref, wqr_ref, wuk_ref,
                 ql_ref, qr_ref, ckv_ref, kr_ref, ckvb_ref, krb_ref):
    xn = _rms(x_ref[...], n1_ref[...]).astype(BF16)
    z = _dot(xn, win_ref[...])
    c_q = _rms(z[:, :Q_LORA], qn_ref[...]).astype(BF16)
    c_kv = _rms(z[:, Q_LORA:Q_LORA + KV_LORA], kvn_ref[...])
    cos = cos_ref[...]
    sin = sin_ref[...]
    kr0 = Q_LORA + KV_LORA
    k_rope = z[:, kr0:kr0 + LANES] * cos + z[:, kr0 + LANES:kr0 + 2 * LANES] * sin
    ckv_ref[...] = c_kv
    kr_ref[...] = k_rope[:, :ROPE_DIM]
    ckvb_ref[...] = c_kv.astype(BF16)
    krb_ref[...] = k_rope.astype(BF16)
    q_nope = _dot(c_q, wuq_ref[...]).astype(BF16)
    qr2 = _dot(c_q, wqr_ref[...])
    hw = N_HEADS * LANES
    cos8 = jnp.concatenate([cos] * N_HEADS, axis=1)
    sin8 = jnp.concatenate([sin] * N_HEADS, axis=1)
    q_rope = qr2[:, :hw] * cos8 + qr2[:, hw:] * sin8
    qr_ref[...] = (q_rope * Q_SCALE).astype(BF16)
    for h in range(N_HEADS):
        q_lat = _dot(q_nope[:, h * NOPE_DIM:(h + 1) * NOPE_DIM], wuk_ref[h])
        ql_ref[:, h * KV_LORA:(h + 1) * KV_LORA] = (q_lat * Q_SCALE).astype(BF16)


def _proj(x, cos_t, sin_t, table_block, p):
    rows = x.shape[0]
    nt = rows // ROW_TILE
    row_spec = lambda w: pl.BlockSpec((ROW_TILE, w), lambda i: (i, 0))
    tab_spec = pl.BlockSpec((ROW_TILE, LANES), lambda i: (table_block(i), 0))
    return pl.pallas_call(
        _proj_kernel,
        grid=(nt,),
        in_specs=[row_spec(D_MODEL), tab_spec, tab_spec,
                  _const_spec((1, D_MODEL)), _const_spec((D_MODEL, SMALL_IN)),
                  _const_spec((1, Q_LORA)), _const_spec((1, KV_LORA)),
                  _const_spec((Q_LORA, N_HEADS * NOPE_DIM)), _const_spec((Q_LORA, 2 * N_HEADS * LANES)),
                  _const_spec((N_HEADS, NOPE_DIM, KV_LORA))],
        out_specs=[row_spec(N_HEADS * KV_LORA), row_spec(N_HEADS * LANES), row_spec(KV_LORA),
                   row_spec(ROPE_DIM), row_spec(KV_LORA), row_spec(LANES)],
        out_shape=[jax.ShapeDtypeStruct((rows, N_HEADS * KV_LORA), BF16),
                   jax.ShapeDtypeStruct((rows, N_HEADS * LANES), BF16),
                   jax.ShapeDtypeStruct((rows, KV_LORA), F32),
                   jax.ShapeDtypeStruct((rows, ROPE_DIM), F32),
                   jax.ShapeDtypeStruct((rows, KV_LORA), BF16),
                   jax.ShapeDtypeStruct((rows, LANES), BF16)],
        compiler_params=pltpu.CompilerParams(dimension_semantics=("parallel",), vmem_limit_bytes=VMEM_LIMIT),
        name="in_proj",
    )(x, cos_t, sin_t, p["norm1_g"], p["w_in_small"], p["q_norm_g"], p["kv_norm_g"], p["w_uq"], p["w_qr2"],
      p["w_uk_t"])


def _prompt_attn_kernel(ql_ref, qr_ref, ckv_ref, kr_ref, o_ref, q_s, qr_s, sa_buf, sb_buf, acc_ref, m_ref,
                        l_ref):
    qi = pl.program_id(1)
    for h in range(N_HEADS):
        q_s[h * Q_TOKENS:(h + 1) * Q_TOKENS, :] = ql_ref[:, h * KV_LORA:(h + 1) * KV_LORA]
        qr_s[h * Q_TOKENS:(h + 1) * Q_TOKENS, :] = qr_ref[:, h * LANES:(h + 1) * LANES]
    m_ref[...] = jnp.full(m_ref.shape, NEG_INF, F32)
    l_ref[...] = jnp.zeros(l_ref.shape, F32)
    acc_ref[...] = jnp.zeros(acc_ref.shape, F32)

    def keys(j):
        off = pl.multiple_of(j * KEY_BLOCK, KEY_BLOCK)
        return ckv_ref[pl.ds(off, KEY_BLOCK), :], kr_ref[pl.ds(off, KEY_BLOCK), :]

    def scores(j, s_buf):
        k, kr = keys(j)
        s_buf[...] = _dot_nt(q_s[...], k) + _dot_nt(qr_s[...], kr)

    def consume(j, s_buf, masked):
        k, _ = keys(j)
        for c in range(N_HEADS * Q_TOKENS // ATTN_ROW_CHUNK):
            rows = pl.ds(c * ATTN_ROW_CHUNK, ATTN_ROW_CHUNK)
            s = s_buf[rows, :]
            if masked:
                row = lax.broadcasted_iota(jnp.int32, s.shape, 0)
                col = lax.broadcasted_iota(jnp.int32, s.shape, 1)
                s = jnp.where(j * KEY_BLOCK + col <= qi * Q_TOKENS + row % Q_TOKENS, s, NEG_INF)
            m_prev = m_ref[rows, :]
            m_new = jnp.maximum(m_prev, jnp.max(s, axis=-1, keepdims=True))
            alpha = jnp.exp2(m_prev - m_new)
            p = jnp.exp2(s - m_new)
            l_ref[rows, :] = alpha * l_ref[rows, :] + jnp.sum(p, axis=-1, keepdims=True)
            acc_ref[rows, :] = acc_ref[rows, :] * jnp.concatenate([alpha, alpha], axis=1) + _dot(p.astype(BF16), k)
            m_ref[rows, :] = m_new

    assert Q_TOKENS == KEY_BLOCK
    scores(0, sa_buf)

    def body(jj, carry):
        j = 2 * jj
        scores(j + 1, sb_buf)
        consume(j, sa_buf, False)
        scores(j + 2, sa_buf)
        consume(j + 1, sb_buf, False)
        return carry

    lax.fori_loop(0, qi // 2, body, 0)

    @pl.when(qi % 2 == 1)
    def _():
        scores(qi, sb_buf)
        consume(qi - 1, sa_buf, False)
        consume(qi, sb_buf, True)

    @pl.when(qi % 2 == 0)
    def _():
        consume(qi, sa_buf, True)

    inv = 1.0 / l_ref[...]
    o = acc_ref[...] * jnp.concatenate([inv, inv], axis=1)
    for h in range(N_HEADS):
        o_ref[:, h * KV_LORA:(h + 1) * KV_LORA] = o[h * Q_TOKENS:(h + 1) * Q_TOKENS, :].astype(BF16)


def _prompt_attn(q_lat, q_rope, ckv_b, kr_b, batch, seq):
    nq = seq // Q_TOKENS
    hq = N_HEADS * Q_TOKENS
    return pl.pallas_call(
        _prompt_attn_kernel,
        grid=(batch, nq),
        in_specs=[pl.BlockSpec((Q_TOKENS, N_HEADS * KV_LORA), lambda b, i: (b * nq + i, 0)),
                  pl.BlockSpec((Q_TOKENS, N_HEADS * LANES), lambda b, i: (b * nq + i, 0)),
                  pl.BlockSpec((seq, KV_LORA), lambda b, i: (b, 0)),
                  pl.BlockSpec((seq, LANES), lambda b, i: (b, 0))],
        out_specs=pl.BlockSpec((Q_TOKENS, N_HEADS * KV_LORA), lambda b, i: (b * nq + i, 0)),
        out_shape=jax.ShapeDtypeStruct((batch * seq, N_HEADS * KV_LORA), BF16),
        scratch_shapes=[pltpu.VMEM((hq, KV_LORA), BF16), pltpu.VMEM((hq, LANES), BF16),
                        pltpu.VMEM((hq, KEY_BLOCK), F32), pltpu.VMEM((hq, KEY_BLOCK), F32),
                        pltpu.VMEM((hq, KV_LORA), F32), pltpu.VMEM((hq, LANES), F32),
                        pltpu.VMEM((hq, LANES), F32)],
        compiler_params=pltpu.CompilerParams(dimension_semantics=("parallel", "arbitrary"),
                                             vmem_limit_bytes=VMEM_LIMIT),
        name="prompt_attn",
    )(q_lat, q_rope, ckv_b, kr_b)


def _sample_attn_kernel(pt_ref, q_ref, qr_ref, ckn_ref, krn_ref, ck_hbm, kr_hbm, o_ref,
                        ck_buf, kr_buf, sems, acc_ref, m_ref, l_ref, *, batch, n_pages, dec_seq):
    npg = PAGES_PER_STEP
    nbuf = PAGE_SLOTS
    n_chunks = n_pages // npg
    b = pl.program_id(0)
    total = batch * n_chunks
    q = q_ref[...]
    qr = qr_ref[...]

    def start_chunk(t, slot):
        for i in range(npg):
            page = pt_ref[t * npg + i]
            pltpu.make_async_copy(ck_hbm.at[page], ck_buf.at[slot, i], sems.at[0, slot]).start()
            pltpu.make_async_copy(kr_hbm.at[page], kr_buf.at[slot, i], sems.at[1, slot]).start()

    def wait_chunk(slot):
        pltpu.make_async_copy(ck_buf.at[slot], ck_buf.at[slot], sems.at[0, slot]).wait()
        pltpu.make_async_copy(kr_buf.at[slot], kr_buf.at[slot], sems.at[1, slot]).wait()

    @pl.when(b == 0)
    def _():
        for t0 in range(nbuf - 1):
            start_chunk(t0, t0)

    kn = ckn_ref[...]
    s = _dot_nt(q, kn) + _dot_nt(qr, krn_ref[...])
    row = lax.broadcasted_iota(jnp.int32, s.shape, 0)
    col = lax.broadcasted_iota(jnp.int32, s.shape, 1)
    s = jnp.where(col <= row % dec_seq, s, NEG_INF)
    m0 = jnp.max(s, axis=-1, keepdims=True)
    p = jnp.exp2(s - m0)
    m_ref[...] = jnp.broadcast_to(m0, m_ref.shape)
    l_ref[...] = jnp.broadcast_to(jnp.sum(p, axis=-1, keepdims=True), l_ref.shape)
    acc_ref[...] = _dot(p.astype(BF16), kn)

    def body(c, carry):
        slot = c % nbuf
        t = b * n_chunks + c

        @pl.when(t + nbuf - 1 < total)
        def _():
            start_chunk(t + nbuf - 1, (c + nbuf - 1) % nbuf)

        wait_chunk(slot)
        ks = [ck_buf[slot, i].astype(BF16) for i in range(npg)]
        ss = [_dot_nt(q, ks[i]) + _dot(qr, kr_buf[slot, i].astype(BF16)) for i in range(npg)]
        m_prev = m_ref[...]
        m_cur = jnp.max(functools.reduce(jnp.maximum, ss), axis=-1, keepdims=True)
        m_new = jnp.maximum(m_prev, m_cur)
        alpha = jnp.exp2(m_prev - m_new)
        ps = [jnp.exp2(s - m_new) for s in ss]
        l_ref[...] = alpha * l_ref[...] + jnp.sum(functools.reduce(jnp.add, ps), axis=-1, keepdims=True)
        pv = functools.reduce(jnp.add, [_dot(p.astype(BF16), k) for p, k in zip(ps, ks)])
        acc_ref[...] = acc_ref[...] * jnp.concatenate([alpha, alpha], axis=1) + pv
        m_ref[...] = m_new
        return carry

    lax.fori_loop(0, n_chunks, body, 0)
    inv = 1.0 / l_ref[...]
    o_ref[...] = (acc_ref[...] * jnp.concatenate([inv, inv], axis=1)).astype(BF16)


def _sample_attn(page_table, q, qr, ck_new, kr_new, cache_ckv, cache_krope_t, dec_seq):
    batch, n_pages = page_table.shape
    npg = PAGES_PER_STEP
    nbuf = PAGE_SLOTS
    assert n_pages % (nbuf * npg) == 0 and batch * (n_pages // npg) >= nbuf
    rows = N_HEADS * dec_seq
    per_b = lambda w, r: pl.BlockSpec((None, r, w), lambda b, pt: (b, 0, 0))
    grid_spec = pltpu.PrefetchScalarGridSpec(
        num_scalar_prefetch=1,
        grid=(batch,),
        in_specs=[per_b(KV_LORA, rows), per_b(ROPE_DIM, rows), per_b(KV_LORA, PAGE_SIZE), per_b(ROPE_DIM, PAGE_SIZE),
                  pl.BlockSpec(memory_space=pl.ANY), pl.BlockSpec(memory_space=pl.ANY)],
        out_specs=per_b(KV_LORA, rows),
        scratch_shapes=[pltpu.VMEM((nbuf, npg, PAGE_SIZE, KV_LORA), F32),
                        pltpu.VMEM((nbuf, npg, ROPE_DIM, PAGE_SIZE), F32),
                        pltpu.SemaphoreType.DMA((2, nbuf)),
                        pltpu.VMEM((rows, KV_LORA), F32), pltpu.VMEM((rows, LANES), F32),
                        pltpu.VMEM((rows, LANES), F32)],
    )
    return pl.pallas_call(
        functools.partial(_sample_attn_kernel, batch=batch, n_pages=n_pages, dec_seq=dec_seq),
        grid_spec=grid_spec,
        out_shape=jax.ShapeDtypeStruct((batch, rows, KV_LORA), BF16),
        compiler_params=pltpu.CompilerParams(dimension_semantics=("arbitrary",), vmem_limit_bytes=VMEM_LIMIT),
        name="sample_attn",
    )(page_table.reshape(-1), q, qr, ck_new, kr_new, cache_ckv, cache_krope_t)


def _route(logits):
    lane = lax.broadcasted_iota(jnp.int32, logits.shape, 1)
    big = jnp.int32(LANES)
    gl = jnp.where(lane < N_GROUPS, logits, NEG_INF)
    gmax = jnp.max(gl, axis=-1, keepdims=True)
    gsum = jnp.sum(jnp.exp(gl - gmax), axis=-1, keepdims=True)
    g_w = 1.0 / gsum
    g_idx = jnp.min(jnp.where(gl == gmax, lane, big), axis=-1, keepdims=True)
    lo = N_GROUPS + g_idx * EXPERTS_PER_GROUP
    el = jnp.where((lane >= lo) & (lane < lo + EXPERTS_PER_GROUP), logits, NEG_INF)
    m1 = jnp.max(el, axis=-1, keepdims=True)
    i1 = jnp.min(jnp.where(el == m1, lane, big), axis=-1, keepdims=True)
    el2 = jnp.where(lane == i1, NEG_INF, el)
    m2 = jnp.max(el2, axis=-1, keepdims=True)
    i2 = jnp.min(jnp.where(el2 == m2, lane, big), axis=-1, keepdims=True)
    esum = jnp.sum(jnp.exp(el - m1), axis=-1, keepdims=True)
    p1 = 1.0 / esum
    p2 = jnp.exp(m2 - m1) / esum
    tot = p1 + p2
    return i1 - N_GROUPS, i2 - N_GROUPS, g_w * (p1 / tot), g_w * (p2 / tot)


def _post_kernel(x_ref, o_ref, n1_ref, win_ref, sg_ref, wuv_ref, ws_ref, bt_ref, wo_ref, n2_ref,
                 wrh_ref, wrl_ref, br_ref, x1_ref, route_ref, *v_out, chunk_rows):
    x = x_ref[...]
    xn = _rms(x, n1_ref[...]).astype(BF16)
    z = _dot(xn, win_ref[...])
    u = jax.nn.gelu(z[:, :D_MODEL])
    v = _rms(jax.nn.gelu(z[:, D_MODEL:2 * D_MODEL]), sg_ref[...])
    if v_out:
        v_out[0][...] = v
    ga = jax.nn.sigmoid(z[:, 2 * D_MODEL:3 * D_MODEL])
    gb = jax.nn.sigmoid(z[:, 3 * D_MODEL:])
    attn = jnp.concatenate(
        [_dot(o_ref[:, h * KV_LORA:(h + 1) * KV_LORA], wuv_ref[h]) for h in range(N_HEADS)], axis=1)
    row = lax.broadcasted_iota(jnp.int32, (CHUNK, CHUNK), 0)
    col = lax.broadcasted_iota(jnp.int32, (CHUNK, CHUNK), 1)
    keep = (col <= row) & (row // chunk_rows == col // chunk_rows)
    vb = v.astype(BF16)
    bt = bt_ref[...]
    mixed_rows = []
    for c in range(ROW_TILE // CHUNK):
        groups = []
        for g in range(SGU_GROUPS):
            ws = jnp.where(keep, ws_ref[g], 0.0).astype(BF16)
            vg = vb[c * CHUNK:(c + 1) * CHUNK, g * SGU_GROUP_WIDTH:(g + 1) * SGU_GROUP_WIDTH]
            groups.append(_dot(ws, vg) + bt[:, g:g + 1])
        mixed_rows.append(jnp.concatenate(groups, axis=1))
    mixed = jnp.concatenate(mixed_rows, axis=0)
    merged = ga * attn + gb * (u * mixed)
    x1 = x + _dot(merged.astype(BF16), wo_ref[...])
    x1_ref[...] = x1
    xt = _rms(x1, n2_ref[...])
    xt_hi = xt.astype(BF16)
    xt_lo = (xt - xt_hi.astype(F32)).astype(BF16)
    logits = (_dot(xt_hi, wrh_ref[...]) + _dot(xt_lo, wrh_ref[...]) + _dot(xt_hi, wrl_ref[...])) + br_ref[...]
    e0, e1, w0, w1 = _route(logits)
    lane = lax.broadcasted_iota(jnp.int32, logits.shape, 1)
    route_ref[...] = jnp.where(lane == 0, e0.astype(F32),
                               jnp.where(lane == 1, e1.astype(F32),
                                         jnp.where(lane == 2, w0, jnp.where(lane == 3, w1, 0.0))))


def _post(x, o_lat, ws_eff, bt_eff, chunk_rows, emit_v, p):
    rows = x.shape[0]
    nt = rows // ROW_TILE
    row_spec = lambda w: pl.BlockSpec((ROW_TILE, w), lambda i: (i, 0))
    out_specs = [row_spec(D_MODEL), row_spec(LANES)]
    out_shape = [jax.ShapeDtypeStruct((rows, D_MODEL), F32), jax.ShapeDtypeStruct((rows, LANES), F32)]
    if emit_v:
        out_specs.append(row_spec(D_MODEL))
        out_shape.append(jax.ShapeDtypeStruct((rows, D_MODEL), F32))
    return pl.pallas_call(
        functools.partial(_post_kernel, chunk_rows=chunk_rows),
        grid=(nt,),
        in_specs=[row_spec(D_MODEL), row_spec(N_HEADS * KV_LORA),
                  _const_spec((1, D_MODEL)), _const_spec((D_MODEL, 4 * D_MODEL)), _const_spec((1, D_MODEL)),
                  _const_spec((N_HEADS, KV_LORA, V_DIM)), _const_spec((SGU_GROUPS, CHUNK, CHUNK)),
                  _const_spec((CHUNK, SGU_GROUPS)), _const_spec((D_MODEL, D_MODEL)), _const_spec((1, D_MODEL)),
                  _const_spec((D_MODEL, LANES)), _const_spec((D_MODEL, LANES)), _const_spec((1, LANES))],
        out_specs=out_specs,
        out_shape=out_shape,
        compiler_params=pltpu.CompilerParams(dimension_semantics=("parallel",), vmem_limit_bytes=VMEM_LIMIT),
        name="merge_route",
    )(x, o_lat, p["norm1_g"], p["w_in_big"], p["sgu_norm_g"], p["w_uv_h"], ws_eff, bt_eff, p["w_o"],
      p["norm2_g"], p["w_r_hi"], p["w_r_lo"], p["b_r"])


def _rank_kernel(route_ref, rank_ref, counts_ref, carry_ref):
    @pl.when(pl.program_id(0) == 0)
    def _():
        carry_ref[...] = jnp.zeros(carry_ref.shape, F32)

    r = route_ref[...]
    lane = lax.broadcasted_iota(jnp.int32, r.shape, 1)
    oh0 = jnp.where(lane == r[:, 0:1].astype(jnp.int32), 1.0, 0.0)
    oh1 = jnp.where(lane == r[:, 1:2].astype(jnp.int32), 1.0, 0.0)
    both = oh0 + oh1
    row = lax.broadcasted_iota(jnp.int32, (ROW_TILE, ROW_TILE), 0)
    col = lax.broadcasted_iota(jnp.int32, (ROW_TILE, ROW_TILE), 1)
    before = jnp.where(col < row, 1.0, 0.0).astype(BF16)
    base = carry_ref[...] + _dot(before, both.astype(BF16))
    rank0 = jnp.sum(oh0 * base, axis=-1, keepdims=True)
    rank1 = jnp.sum(oh1 * (base + oh0), axis=-1, keepdims=True)
    rank_ref[...] = jnp.where(lane == 0, rank0, jnp.where(lane == 1, rank1, 0.0))
    carry_ref[...] = carry_ref[...] + jnp.sum(both, axis=0, keepdims=True)
    counts_ref[...] = carry_ref[...]


def _rank(route):
    rows = route.shape[0]
    return pl.pallas_call(
        _rank_kernel,
        grid=(rows // ROW_TILE,),
        in_specs=[pl.BlockSpec((ROW_TILE, LANES), lambda i: (i, 0))],
        out_specs=[pl.BlockSpec((ROW_TILE, LANES), lambda i: (i, 0)), _const_spec((1, LANES))],
        out_shape=[jax.ShapeDtypeStruct((rows, LANES), F32), jax.ShapeDtypeStruct((1, LANES), F32)],
        scratch_shapes=[pltpu.VMEM((1, LANES), F32)],
        compiler_params=pltpu.CompilerParams(dimension_semantics=("arbitrary",)),
        name="expert_rank",
    )(route)


def _row_copy(src, src_row, dst, dst_row, sem):
    return pltpu.make_async_copy(src.at[pl.ds(src_row, 1)], dst.at[pl.ds(dst_row, 1)], sem)


def _dispatch_kernel(pos_ref, x1_ref, n2_ref, xs_in_ref, xs_ref, xt_ref, sems, *, n_tiles):
    del xs_in_ref
    i = pl.program_id(0)
    slot = i % 2
    xt_ref[slot] = _rms(x1_ref[...], n2_ref[...])

    def issue(r, carry):
        for k in range(TOP_K):
            _row_copy(xt_ref.at[slot], r, xs_ref, pos_ref[0, 0, TOP_K * r + k], sems.at[slot]).start()
        return carry

    lax.fori_loop(0, ROW_TILE, issue, 0, unroll=8)

    def drain(s):
        for k in range(TOP_K):
            pltpu.make_async_copy(xt_ref.at[s], xt_ref.at[s], sems.at[s]).wait()

    @pl.when(i > 0)
    def _():
        drain(1 - slot)

    @pl.when(i == n_tiles - 1)
    def _():
        drain(slot)


def _dispatch(pos3, x1, norm2_g, xs):
    rows = x1.shape[0]
    return pl.pallas_call(
        functools.partial(_dispatch_kernel, n_tiles=rows // ROW_TILE),
        grid=(rows // ROW_TILE,),
        in_specs=[pl.BlockSpec((1, 1, TOP_K * ROW_TILE), lambda i: (i, 0, 0), memory_space=pltpu.SMEM),
                  pl.BlockSpec((ROW_TILE, D_MODEL), lambda i: (i, 0)),
                  _const_spec((1, D_MODEL)),
                  pl.BlockSpec(memory_space=pl.ANY)],
        out_specs=pl.BlockSpec(memory_space=pl.ANY),
        out_shape=jax.ShapeDtypeStruct(xs.shape, xs.dtype),
        scratch_shapes=[pltpu.VMEM((2, ROW_TILE, D_MODEL), F32), pltpu.SemaphoreType.DMA((2,))],
        input_output_aliases={3: 0},
        compiler_params=pltpu.CompilerParams(dimension_semantics=("arbitrary",), has_side_effects=True),
        name="dispatch",
    )(pos3, x1, norm2_g, xs)


def _expert_kernel(be_ref, nb_ref, xs_ref, wg_ref, wu_ref, wd_ref, ys_ref, wg_s, wu_s, wd_s):
    i = pl.program_id(0)
    prev = be_ref[jnp.maximum(i - 1, 0)]

    @pl.when((i == 0) | (be_ref[i] != prev))
    def _():
        wg_s[...] = wg_ref[...].astype(BF16)
        wu_s[...] = wu_ref[...].astype(BF16)
        wd_s[...] = wd_ref[...].astype(BF16)

    @pl.when(i < nb_ref[0])
    def _():
        xb = xs_ref[...].astype(BF16)
        h = jax.nn.silu(_dot(xb, wg_s[...])) * _dot(xb, wu_s[...])
        ys_ref[...] = _dot(h.astype(BF16), wd_s[...])

    @pl.when(i >= nb_ref[0])
    def _():
        ys_ref[...] = jnp.zeros(ys_ref.shape, F32)


def _experts(block_expert, n_used, xs, w_gate, w_up, w_down):
    n_blocks = xs.shape[0] // ROW_BLOCK
    grid_spec = pltpu.PrefetchScalarGridSpec(
        num_scalar_prefetch=2,
        grid=(n_blocks,),
        in_specs=[pl.BlockSpec((ROW_BLOCK, D_MODEL), lambda i, be, nb: (i, 0)),
                  pl.BlockSpec((None, D_MODEL, D_EXPERT), lambda i, be, nb: (be[i], 0, 0)),
                  pl.BlockSpec((None, D_MODEL, D_EXPERT), lambda i, be, nb: (be[i], 0, 0)),
                  pl.BlockSpec((None, D_EXPERT, D_MODEL), lambda i, be, nb: (be[i], 0, 0))],
        out_specs=pl.BlockSpec((ROW_BLOCK, D_MODEL), lambda i, be, nb: (i, 0)),
        scratch_shapes=[pltpu.VMEM((D_MODEL, D_EXPERT), BF16), pltpu.VMEM((D_MODEL, D_EXPERT), BF16),
                        pltpu.VMEM((D_EXPERT, D_MODEL), BF16)],
    )
    return pl.pallas_call(
        _expert_kernel,
        grid_spec=grid_spec,
        out_shape=jax.ShapeDtypeStruct(xs.shape, F32),
        compiler_params=pltpu.CompilerParams(dimension_semantics=("arbitrary",), vmem_limit_bytes=VMEM_LIMIT),
        name="expert_ffn",
    )(block_expert, n_used, xs, w_gate, w_up, w_down)


def _combine_kernel(pos_ref, pos_next_ref, x1_ref, route_ref, nf_ref, ys_ref, y_ref, buf_ref, sems, *, n_tiles):
    i = pl.program_id(0)
    slot = i % 2

    def gather(p_ref, s):
        def issue(r, carry):
            for k in range(TOP_K):
                _row_copy(ys_ref, p_ref[0, 0, TOP_K * r + k], buf_ref.at[s, k], r, sems.at[s]).start()
            return carry

        lax.fori_loop(0, ROW_TILE, issue, 0, unroll=8)

    @pl.when(i == 0)
    def _():
        gather(pos_ref, 0)

    @pl.when(i + 1 < n_tiles)
    def _():
        gather(pos_next_ref, 1 - slot)

    pltpu.make_async_copy(buf_ref.at[slot], buf_ref.at[slot], sems.at[slot]).wait()
    route = route_ref[...]
    y = x1_ref[...] + buf_ref[slot, 0] * route[:, 2:3] + buf_ref[slot, 1] * route[:, 3:4]
    y_ref[...] = _rms(y, nf_ref[...])


def _combine(pos3, x1, route, norm_f_g, ys):
    rows = x1.shape[0]
    nt = rows // ROW_TILE
    return pl.pallas_call(
        functools.partial(_combine_kernel, n_tiles=nt),
        grid=(nt,),
        in_specs=[pl.BlockSpec((1, 1, TOP_K * ROW_TILE), lambda i: (i, 0, 0), memory_space=pltpu.SMEM),
                  pl.BlockSpec((1, 1, TOP_K * ROW_TILE), lambda i: (jnp.minimum(i + 1, nt - 1), 0, 0),
                               memory_space=pltpu.SMEM),
                  pl.BlockSpec((ROW_TILE, D_MODEL), lambda i: (i, 0)),
                  pl.BlockSpec((ROW_TILE, LANES), lambda i: (i, 0)),
                  _const_spec((1, D_MODEL)),
                  pl.BlockSpec(memory_space=pl.ANY)],
        out_specs=pl.BlockSpec((ROW_TILE, D_MODEL), lambda i: (i, 0)),
        out_shape=jax.ShapeDtypeStruct((rows, D_MODEL), F32),
        scratch_shapes=[pltpu.VMEM((2, TOP_K, ROW_TILE, D_MODEL), F32), pltpu.SemaphoreType.DMA((2,))],
        compiler_params=pltpu.CompilerParams(dimension_semantics=("arbitrary",)),
        name="combine",
    )(pos3, pos3, x1, route, norm_f_g, ys)


def _prepare_params(norm1_g, w_in, q_norm_g, kv_norm_g, w_uq, w_qr, w_uk, w_uv, sgu_norm_g, w_o, norm2_g,
                    w_rg, b_rg, w_re, b_re):
    def rot(w):
        return jnp.concatenate([-w[..., ROPE_HALF:], w[..., :ROPE_HALF]], axis=-1)

    q0, kv0, kr0 = 0, Q_LORA, Q_LORA + KV_LORA
    big0 = kr0 + ROPE_DIM
    w_kr = w_in[:, kr0:big0]
    w_in_small = jnp.concatenate([w_in[:, q0:kr0], w_kr, w_kr, rot(w_kr), rot(w_kr)], axis=1).astype(BF16)
    zeros = jnp.zeros((Q_LORA, N_HEADS, LANES - ROPE_DIM), w_qr.dtype)
    pad = lambda w: jnp.concatenate([w, zeros], axis=-1).reshape(Q_LORA, N_HEADS * LANES)
    w_qr2 = jnp.concatenate([pad(w_qr), pad(rot(w_qr))], axis=1).astype(BF16)
    w_r = jnp.concatenate([w_rg, w_re, jnp.zeros((D_MODEL, LANES - N_GROUPS - N_EXPERTS), F32)], axis=1)
    w_r_hi = w_r.astype(BF16)
    b_r = jnp.concatenate([b_rg, b_re, jnp.zeros((LANES - N_GROUPS - N_EXPERTS,), F32)])[None, :]
    return dict(
        norm1_g=norm1_g[None, :], w_in_small=w_in_small, w_in_big=w_in[:, big0:].astype(BF16),
        q_norm_g=q_norm_g[None, :], kv_norm_g=kv_norm_g[None, :],
        w_uq=w_uq.reshape(Q_LORA, N_HEADS * NOPE_DIM).astype(BF16), w_qr2=w_qr2,
        w_uk_t=jnp.transpose(w_uk, (1, 2, 0)).astype(BF16), w_uv_h=jnp.transpose(w_uv, (1, 0, 2)).astype(BF16),
        sgu_norm_g=sgu_norm_g[None, :], w_o=w_o.astype(BF16), norm2_g=norm2_g[None, :],
        w_r_hi=w_r_hi, w_r_lo=(w_r - w_r_hi.astype(F32)).astype(BF16), b_r=b_r)


def _moe_and_final_norm(x1s, routes, norm2_g, w_gate, w_up, w_down, norm_f_g):
    route = jnp.concatenate(routes, axis=0)
    rank, counts = _rank(route)
    expert = route[:, :TOP_K].astype(jnp.int32)
    sizes = counts[0, :N_EXPERTS].astype(jnp.int32)
    padded = (sizes + ROW_BLOCK - 1) // ROW_BLOCK * ROW_BLOCK
    pad_end = jnp.cumsum(padded)
    pad_start = pad_end - padded
    start_of = jnp.sum(jnp.where(expert[..., None] == jnp.arange(N_EXPERTS, dtype=jnp.int32), pad_start, 0), axis=-1)
    pos = start_of + rank[:, :TOP_K].astype(jnp.int32)
    n_rows = route.shape[0]
    n_blocks = -(-(n_rows * TOP_K) // ROW_BLOCK) + N_EXPERTS
    block_start = jnp.arange(n_blocks, dtype=jnp.int32) * ROW_BLOCK
    block_expert = jnp.minimum(jnp.sum(pad_end[None, :] <= block_start[:, None], axis=1), N_EXPERTS - 1).astype(jnp.int32)
    n_used = (pad_end[-1:] // ROW_BLOCK).astype(jnp.int32)
    pos3 = pos.reshape(n_rows // ROW_TILE, 1, TOP_K * ROW_TILE)
    tile_ends = [0]
    for x1 in x1s:
        tile_ends.append(tile_ends[-1] + x1.shape[0] // ROW_TILE)
    pos3s = [pos3[a:b] for a, b in zip(tile_ends[:-1], tile_ends[1:])]

    xs = jnp.zeros((n_blocks * ROW_BLOCK, D_MODEL), F32)
    for x1, p3 in zip(x1s, pos3s):
        xs = _dispatch(p3, x1, norm2_g, xs)
    ys = _experts(block_expert, n_used, xs, w_gate, w_up, w_down)
    return [_combine(p3, x1, r, norm_f_g, ys) for x1, r, p3 in zip(x1s, routes, pos3s)]


def kernel(x_prompt, x_sample, cache_ckv, cache_krope, page_table, norm1_g, w_in, q_norm_g, kv_norm_g, w_uq, w_qr,
           w_uk, w_uv, sgu_norm_g, w_s, b_s, w_o, norm2_g, w_rg, b_rg, w_re, b_re, w_gate, w_up, w_down, norm_f_g):
    b_p, s_p, _ = x_prompt.shape
    b_d, t_d, _ = x_sample.shape
    depth = norm1_g.shape[0]
    past_len = page_table.shape[1] * PAGE_SIZE
    rows_p, rows_d = b_p * s_p, b_d * t_d
    assert depth == 1 and s_p % ROW_TILE == 0 and rows_d % ROW_TILE == 0 and CHUNK % t_d == 0
    cos_t, sin_t = _rope_tables(s_p, past_len, t_d)
    tiles_per_seq = s_p // ROW_TILE

    xp = x_prompt.reshape(rows_p, D_MODEL)
    xd = x_sample.reshape(rows_d, D_MODEL)
    l = 0
    p = _prepare_params(norm1_g[l], w_in[l], q_norm_g[l], kv_norm_g[l], w_uq[l], w_qr[l], w_uk[l], w_uv[l],
                        sgu_norm_g[l], w_o[l], norm2_g[l], w_rg[l], b_rg[l], w_re[l], b_re[l])

    ql_p, qr_p, ckv_p, kr_p, ckvb_p, krb_p = _proj(xp, cos_t, sin_t, lambda i: i % tiles_per_seq, p)
    o_p = _prompt_attn(ql_p, qr_p, ckvb_p, krb_p, b_p, s_p)
    bt_p = jnp.transpose(b_s[l])
    x1_p, route_p = _post(xp, o_p, w_s[l], bt_p, CHUNK, False, p)

    ql_d, qr_d, ckv_d, kr_d, ckvb_d, krb_d = _proj(xd, cos_t, sin_t, lambda i: tiles_per_seq, p)
    hr = N_HEADS * t_d
    q_d = ql_d.reshape(b_d, t_d, N_HEADS, KV_LORA).transpose(0, 2, 1, 3).reshape(b_d, hr, KV_LORA)
    qrope_d = qr_d.reshape(b_d, t_d, N_HEADS, LANES)[..., :ROPE_DIM].transpose(0, 2, 1, 3).reshape(b_d, hr, ROPE_DIM)
    pad_rows = lambda a: jnp.pad(a.reshape(b_d, t_d, -1), ((0, 0), (0, PAGE_SIZE - t_d), (0, 0)))
    o_d = _sample_attn(page_table, q_d, qrope_d, pad_rows(ckvb_d), pad_rows(krb_d[:, :ROPE_DIM]),
                       cache_ckv[l], jnp.swapaxes(cache_krope[l], 1, 2), t_d)
    o_d = o_d.reshape(b_d, N_HEADS, t_d, KV_LORA).transpose(0, 2, 1, 3).reshape(rows_d, N_HEADS * KV_LORA)
    reps = CHUNK // t_d
    ws_d = jnp.tile(w_s[l][:, :t_d, :t_d], (1, reps, reps))
    bt_d = jnp.transpose(jnp.tile(b_s[l][:, :t_d], (1, reps)))
    x1_d, route_d, v_d = _post(xd, o_d, ws_d, bt_d, t_d, True, p)

    y_p, y_d = _moe_and_final_norm([x1_p, x1_d], [route_p, route_d], p["norm2_g"], w_gate[l], w_up[l], w_down[l],
                                   norm_f_g[None, :])
    n_pp = s_p // PAGE_SIZE
    return (y_p.reshape(b_p, s_p, D_MODEL),
            y_d.reshape(b_d, t_d, D_MODEL),
            ckv_p.reshape(depth, b_p, n_pp, PAGE_SIZE, KV_LORA),
            kr_p.reshape(depth, b_p, n_pp, PAGE_SIZE, ROPE_DIM),
            ckv_d.reshape(depth, b_d, t_d, KV_LORA),
            kr_d.reshape(depth, b_d, t_d, ROPE_DIM),
            v_d.reshape(depth, b_d, t_d, D_MODEL))
```

```python
import functools
import math

import jax
import jax.numpy as jnp
from jax import lax
from jax.experimental import pallas as pl
from jax.experimental.pallas import tpu as pltpu

F32 = jnp.float32
BF16 = jnp.bfloat16

D_MODEL = 1024
PAGE_SIZE = 128
N_HEADS = 8
V_DIM = 128
Q_LORA = 384
KV_LORA = 256
NOPE_DIM = 128
ROPE_DIM = 64
ROPE_HALF = ROPE_DIM // 2
ROPE_THETA = 10000.0
ATTN_SCALE = (NOPE_DIM + ROPE_DIM) ** -0.5
Q_SCALE = ATTN_SCALE * math.log2(math.e)
CHUNK = 128
SGU_GROUPS = 8
SGU_GROUP_WIDTH = 128
N_GROUPS = 8
EXPERTS_PER_GROUP = 8
N_EXPERTS = N_GROUPS * EXPERTS_PER_GROUP
TOP_K = 2
D_EXPERT = 256
ROW_BLOCK = 256
EPS = 1e-6

LANES = 128
ROW_TILE = 256
Q_TOKENS = 128
KEY_BLOCK = 128
PAGES_PER_STEP = 32
PAGE_SLOTS = 4
ATTN_ROW_CHUNK = 256
VMEM_LIMIT = 48 * 1024 * 1024
SMALL_IN = Q_LORA + KV_LORA + 2 * LANES
NEG_INF = float("-inf")


def _rms(x, g):
    return x * lax.rsqrt(jnp.mean(x * x, axis=-1, keepdims=True) + EPS) * g


def _dot(a, b):
    return jnp.dot(a, b, preferred_element_type=F32)


def _dot_nt(a, b):
    return lax.dot_general(a, b, (((1,), (1,)), ((), ())), preferred_element_type=F32)


def _const_spec(shape):
    nd = len(shape)
    return pl.BlockSpec(shape, lambda *_: (0,) * nd)


def _rope_table_kernel(cos_ref, sin_ref, *, seq, past_len, dec_seq):
    n = cos_ref.shape[0]
    row = lax.broadcasted_iota(jnp.int32, (n, LANES), 0)
    lane = lax.broadcasted_iota(jnp.int32, (n, LANES), 1)
    pos = jnp.where(row < seq, row, past_len + (row - seq) % dec_seq)
    j = (lane % ROPE_HALF).astype(F32)
    freq = jnp.power(jnp.full(j.shape, ROPE_THETA, F32), j * (-1.0 / ROPE_HALF))
    ang = pos.astype(F32) * freq
    cos_ref[...] = jnp.cos(ang)
    sin_ref[...] = jnp.sin(ang)


def _rope_tables(seq, past_len, dec_seq):
    n = seq + ROW_TILE
    return pl.pallas_call(
        functools.partial(_rope_table_kernel, seq=seq, past_len=past_len, dec_seq=dec_seq),
        out_shape=(jax.ShapeDtypeStruct((n, LANES), F32), jax.ShapeDtypeStruct((n, LANES), F32)),
        name="rope_table",
    )()


def _proj_kernel(x_ref, cos_ref, sin_ref, n1_ref, win_ref, qn_ref, kvn_ref, wuq_ref, wqr_ref, wuk_ref,
                 ql_ref, qr_ref, ckv_ref, kr_ref, ckvb_ref, krb_ref):
    xn = _rms(x_ref[...], n1_ref[...]).astype(BF16)
    z = _dot(xn, win_ref[...])
    c_q = _rms(z[:, :Q_LORA], qn_ref[...]).astype(BF16)
    c_kv = _rms(z[:, Q_LORA:Q_LORA + KV_LORA], kvn_ref[...])
    cos = cos_ref[...]
    sin = sin_ref[...]
    kr0 = Q_LORA + KV_LORA
    k_rope = z[:, kr0:kr0 + LANES] * cos + z[:, kr0 + LANES:kr0 + 2 * LANES] * sin
    ckv_ref[...] = c_kv
    kr_ref[...] = k_rope[:, :ROPE_DIM]
    ckvb_ref[...] = c_kv.astype(BF16)
    krb_ref[...] = k_rope.astype(BF16)
    q_nope = _dot(c_q, wuq_ref[...]).astype(BF16)
    qr2 = _dot(c_q, wqr_ref[...])
    hw = N_HEADS * LANES
    cos8 = jnp.concatenate([cos] * N_HEADS, axis=1)
    sin8 = jnp.concatenate([sin] * N_HEADS, axis=1)
    q_rope = qr2[:, :hw] * cos8 + qr2[:, hw:] * sin8
    qr_ref[...] = (q_rope * Q_SCALE).astype(BF16)
    for h in range(N_HEADS):
        q_lat = _dot(q_nope[:, h * NOPE_DIM:(h + 1) * NOPE_DIM], wuk_ref[h])
        ql_ref[:, h * KV_LORA:(h + 1) * KV_LORA] = (q_lat * Q_SCALE).astype(BF16)


def _proj(x, cos_t, sin_t, table_block, p):
    rows = x.shape[0]
    nt = rows // ROW_TILE
    row_spec = lambda w: pl.BlockSpec((ROW_TILE, w), lambda i: (i, 0))
    tab_spec = pl.BlockSpec((ROW_TILE, LANES), lambda i: (table_block(i), 0))
    return pl.pallas_call(
        _proj_kernel,
        grid=(nt,),
        in_specs=[row_spec(D_MODEL), tab_spec, tab_spec,
                  _const_spec((1, D_MODEL)), _const_spec((D_MODEL, SMALL_IN)),
                  _const_spec((1, Q_LORA)), _const_spec((1, KV_LORA)),
                  _const_spec((Q_LORA, N_HEADS * NOPE_DIM)), _const_spec((Q_LORA, 2 * N_HEADS * LANES)),
                  _const_spec((N_HEADS, NOPE_DIM, KV_LORA))],
        out_specs=[row_spec(N_HEADS * KV_LORA), row_spec(N_HEADS * LANES), row_spec(KV_LORA),
                   row_spec(ROPE_DIM), row_spec(KV_LORA), row_spec(LANES)],
        out_shape=[jax.ShapeDtypeStruct((rows, N_HEADS * KV_LORA), BF16),
                   jax.ShapeDtypeStruct((rows, N_HEADS * LANES), BF16),
                   jax.ShapeDtypeStruct((rows, KV_LORA), F32),
                   jax.ShapeDtypeStruct((rows, ROPE_DIM), F32),
                   jax.ShapeDtypeStruct((rows, KV_LORA), BF16),
                   jax.ShapeDtypeStruct((rows, LANES), BF16)],
        compiler_params=pltpu.CompilerParams(dimension_semantics=("parallel",), vmem_limit_bytes=VMEM_LIMIT),
        name="in_proj",
    )(x, cos_t, sin_t, p["norm1_g"], p["w_in_small"], p["q_norm_g"], p["kv_norm_g"], p["w_uq"], p["w_qr2"],
      p["w_uk_t"])


def _prompt_attn_kernel(ql_ref, qr_ref, ckv_ref, kr_ref, o_ref, q_s, qr_s, sa_buf, sb_buf, acc_ref, m_ref,
                        l_ref):
    qi = pl.program_id(1)
    for h in range(N_HEADS):
        q_s[h * Q_TOKENS:(h + 1) * Q_TOKENS, :] = ql_ref[:, h * KV_LORA:(h + 1) * KV_LORA]
        qr_s[h * Q_TOKENS:(h + 1) * Q_TOKENS, :] = qr_ref[:, h * LANES:(h + 1) * LANES]
    m_ref[...] = jnp.full(m_ref.shape, NEG_INF, F32)
    l_ref[...] = jnp.zeros(l_ref.shape, F32)
    acc_ref[...] = jnp.zeros(acc_ref.shape, F32)

    def keys(j):
        off = pl.multiple_of(j * KEY_BLOCK, KEY_BLOCK)
        return ckv_ref[pl.ds(off, KEY_BLOCK), :], kr_ref[pl.ds(off, KEY_BLOCK), :]

    def scores(j, s_buf):
        k, kr = keys(j)
        s_buf[...] = _dot_nt(q_s[...], k) + _dot_nt(qr_s[...], kr)

    def consume(j, s_buf, masked):
        k, _ = keys(j)
        for c in range(N_HEADS * Q_TOKENS // ATTN_ROW_CHUNK):
            rows = pl.ds(c * ATTN_ROW_CHUNK, ATTN_ROW_CHUNK)
            s = s_buf[rows, :]
            if masked:
                row = lax.broadcasted_iota(jnp.int32, s.shape, 0)
                col = lax.broadcasted_iota(jnp.int32, s.shape, 1)
                s = jnp.where(j * KEY_BLOCK + col <= qi * Q_TOKENS + row % Q_TOKENS, s, NEG_INF)
            m_prev = m_ref[rows, :]
            m_new = jnp.maximum(m_prev, jnp.max(s, axis=-1, keepdims=True))
            alpha = jnp.exp2(m_prev - m_new)
            p = jnp.exp2(s - m_new)
            l_ref[rows, :] = alpha * l_ref[rows, :] + jnp.sum(p, axis=-1, keepdims=True)
            acc_ref[rows, :] = acc_ref[rows, :] * jnp.concatenate([alpha, alpha], axis=1) + _dot(p.astype(BF16), k)
            m_ref[rows, :] = m_new

    assert Q_TOKENS == KEY_BLOCK
    scores(0, sa_buf)

    def body(jj, carry):
        j = 2 * jj
        scores(j + 1, sb_buf)
        consume(j, sa_buf, False)
        scores(j + 2, sa_buf)
        consume(j + 1, sb_buf, False)
        return carry

    lax.fori_loop(0, qi // 2, body, 0)

    @pl.when(qi % 2 == 1)
    def _():
        scores(qi, sb_buf)
        consume(qi - 1, sa_buf, False)
        consume(qi, sb_buf, True)

    @pl.when(qi % 2 == 0)
    def _():
        consume(qi, sa_buf, True)

    inv = 1.0 / l_ref[...]
    o = acc_ref[...] * jnp.concatenate([inv, inv], axis=1)
    for h in range(N_HEADS):
        o_ref[:, h * KV_LORA:(h + 1) * KV_LORA] = o[h * Q_TOKENS:(h + 1) * Q_TOKENS, :].astype(BF16)


def _prompt_attn(q_lat, q_rope, ckv_b, kr_b, batch, seq):
    nq = seq // Q_TOKENS
    hq = N_HEADS * Q_TOKENS
    return pl.pallas_call(
        _prompt_attn_kernel,
        grid=(batch, nq),
        in_specs=[pl.BlockSpec((Q_TOKENS, N_HEADS * KV_LORA), lambda b, i: (b * nq + i, 0)),
                  pl.BlockSpec((Q_TOKENS, N_HEADS * LANES), lambda b, i: (b * nq + i, 0)),
                  pl.BlockSpec((seq, KV_LORA), lambda b, i: (b, 0)),
                  pl.BlockSpec((seq, LANES), lambda b, i: (b, 0))],
        out_specs=pl.BlockSpec((Q_TOKENS, N_HEADS * KV_LORA), lambda b, i: (b * nq + i, 0)),
        out_shape=jax.ShapeDtypeStruct((batch * seq, N_HEADS * KV_LORA), BF16),
        scratch_shapes=[pltpu.VMEM((hq, KV_LORA), BF16), pltpu.VMEM((hq, LANES), BF16),
                        pltpu.VMEM((hq, KEY_BLOCK), F32), pltpu.VMEM((hq, KEY_BLOCK), F32),
                        pltpu.VMEM((hq, KV_LORA), F32), pltpu.VMEM((hq, LANES), F32),
                        pltpu.VMEM((hq, LANES), F32)],
        compiler_params=pltpu.CompilerParams(dimension_semantics=("parallel", "arbitrary"),
                                             vmem_limit_bytes=VMEM_LIMIT),
        name="prompt_attn",
    )(q_lat, q_rope, ckv_b, kr_b)


def _sample_attn_kernel(pt_ref, q_ref, qr_ref, ckn_ref, krn_ref, ck_hbm, kr_hbm, o_ref,
                        ck_buf, kr_buf, sems, acc_ref, m_ref, l_ref, *, batch, n_pages, dec_seq):
    npg = PAGES_PER_STEP
    nbuf = PAGE_SLOTS
    n_chunks = n_pages // npg
    b = pl.program_id(0)
    total = batch * n_chunks
    q = q_ref[...]
    qr = qr_ref[...]

    def start_chunk(t, slot):
        for i in range(npg):
            page = pt_ref[t * npg + i]
            pltpu.make_async_copy(ck_hbm.at[page], ck_buf.at[slot, i], sems.at[0, slot]).start()
            pltpu.make_async_copy(kr_hbm.at[page], kr_buf.at[slot, i], sems.at[1, slot]).start()

    def wait_chunk(slot):
        pltpu.make_async_copy(ck_buf.at[slot], ck_buf.at[slot], sems.at[0, slot]).wait()
        pltpu.make_async_copy(kr_buf.at[slot], kr_buf.at[slot], sems.at[1, slot]).wait()

    @pl.when(b == 0)
    def _():
        for t0 in range(nbuf - 1):
            start_chunk(t0, t0)

    kn = ckn_ref[...]
    s = _dot_nt(q, kn) + _dot_nt(qr, krn_ref[...])
    row = lax.broadcasted_iota(jnp.int32, s.shape, 0)
    col = lax.broadcasted_iota(jnp.int32, s.shape, 1)
    s = jnp.where(col <= row % dec_seq, s, NEG_INF)
    m0 = jnp.max(s, axis=-1, keepdims=True)
    p = jnp.exp2(s - m0)
    m_ref[...] = jnp.broadcast_to(m0, m_ref.shape)
    l_ref[...] = jnp.broadcast_to(jnp.sum(p, axis=-1, keepdims=True), l_ref.shape)
    acc_ref[...] = _dot(p.astype(BF16), kn)

    def body(c, carry):
        slot = c % nbuf
        t = b * n_chunks + c

        @pl.when(t + nbuf - 1 < total)
        def _():
            start_chunk(t + nbuf - 1, (c + nbuf - 1) % nbuf)

        wait_chunk(slot)
        ks = [ck_buf[slot, i].astype(BF16) for i in range(npg)]
        ss = [_dot_nt(q, ks[i]) + _dot(qr, kr_buf[slot, i].astype(BF16)) for i in range(npg)]
        m_prev = m_ref[...]
        m_cur = jnp.max(functools.reduce(jnp.maximum, ss), axis=-1, keepdims=True)
        m_new = jnp.maximum(m_prev, m_cur)
        alpha = jnp.exp2(m_prev - m_new)
        ps = [jnp.exp2(s - m_new) for s in ss]
        l_ref[...] = alpha * l_ref[...] + jnp.sum(functools.reduce(jnp.add, ps), axis=-1, keepdims=True)
        pv = functools.reduce(jnp.add, [_dot(p.astype(BF16), k) for p, k in zip(ps, ks)])
        acc_ref[...] = acc_ref[...] * jnp.concatenate([alpha, alpha], axis=1) + pv
        m_ref[...] = m_new
        return carry

    lax.fori_loop(0, n_chunks, body, 0)
    inv = 1.0 / l_ref[...]
    o_ref[...] = (acc_ref[...] * jnp.concatenate([inv, inv], axis=1)).astype(BF16)


def _sample_attn(page_table, q, qr, ck_new, kr_new, cache_ckv, cache_krope_t, dec_seq):
    batch, n_pages = page_table.shape
    npg = PAGES_PER_STEP
    nbuf = PAGE_SLOTS
    assert n_pages % (nbuf * npg) == 0 and batch * (n_pages // npg) >= nbuf
    rows = N_HEADS * dec_seq
    per_b = lambda w, r: pl.BlockSpec((None, r, w), lambda b, pt: (b, 0, 0))
    grid_spec = pltpu.PrefetchScalarGridSpec(
        num_scalar_prefetch=1,
        grid=(batch,),
        in_specs=[per_b(KV_LORA, rows), per_b(ROPE_DIM, rows), per_b(KV_LORA, PAGE_SIZE), per_b(ROPE_DIM, PAGE_SIZE),
                  pl.BlockSpec(memory_space=pl.ANY), pl.BlockSpec(memory_space=pl.ANY)],
        out_specs=per_b(KV_LORA, rows),
        scratch_shapes=[pltpu.VMEM((nbuf, npg, PAGE_SIZE, KV_LORA), F32),
                        pltpu.VMEM((nbuf, npg, ROPE_DIM, PAGE_SIZE), F32),
                        pltpu.SemaphoreType.DMA((2, nbuf)),
                        pltpu.VMEM((rows, KV_LORA), F32), pltpu.VMEM((rows, LANES), F32),
                        pltpu.VMEM((rows, LANES), F32)],
    )
    return pl.pallas_call(
        functools.partial(_sample_attn_kernel, batch=batch, n_pages=n_pages, dec_seq=dec_seq),
        grid_spec=grid_spec,
        out_shape=jax.ShapeDtypeStruct((batch, rows, KV_LORA), BF16),
        compiler_params=pltpu.CompilerParams(dimension_semantics=("arbitrary",), vmem_limit_bytes=VMEM_LIMIT),
        name="sample_attn",
    )(page_table.reshape(-1), q, qr, ck_new, kr_new, cache_ckv, cache_krope_t)


def _route(logits):
    lane = lax.broadcasted_iota(jnp.int32, logits.shape, 1)
    big = jnp.int32(LANES)
    gl = jnp.where(lane < N_GROUPS, logits, NEG_INF)
    gmax = jnp.max(gl, axis=-1, keepdims=True)
    gsum = jnp.sum(jnp.exp(gl - gmax), axis=-1, keepdims=True)
    g_w = 1.0 / gsum
    g_idx = jnp.min(jnp.where(gl == gmax, lane, big), axis=-1, keepdims=True)
    lo = N_GROUPS + g_idx * EXPERTS_PER_GROUP
    el = jnp.where((lane >= lo) & (lane < lo + EXPERTS_PER_GROUP), logits, NEG_INF)
    m1 = jnp.max(el, axis=-1, keepdims=True)
    i1 = jnp.min(jnp.where(el == m1, lane, big), axis=-1, keepdims=True)
    el2 = jnp.where(lane == i1, NEG_INF, el)
    m2 = jnp.max(el2, axis=-1, keepdims=True)
    i2 = jnp.min(jnp.where(el2 == m2, lane, big), axis=-1, keepdims=True)
    esum = jnp.sum(jnp.exp(el - m1), axis=-1, keepdims=True)
    p1 = 1.0 / esum
    p2 = jnp.exp(m2 - m1) / esum
    tot = p1 + p2
    return i1 - N_GROUPS, i2 - N_GROUPS, g_w * (p1 / tot), g_w * (p2 / tot)


def _post_kernel(x_ref, o_ref, n1_ref, win_ref, sg_ref, wuv_ref, ws_ref, bt_ref, wo_ref, n2_ref,
                 wrh_ref, wrl_ref, br_ref, x1_ref, route_ref, *v_out, chunk_rows):
    x = x_ref[...]
    xn = _rms(x, n1_ref[...]).astype(BF16)
    z = _dot(xn, win_ref[...])
    u = jax.nn.gelu(z[:, :D_MODEL])
    v = _rms(jax.nn.gelu(z[:, D_MODEL:2 * D_MODEL]), sg_ref[...])
    if v_out:
        v_out[0][...] = v
    ga = jax.nn.sigmoid(z[:, 2 * D_MODEL:3 * D_MODEL])
    gb = jax.nn.sigmoid(z[:, 3 * D_MODEL:])
    attn = jnp.concatenate(
        [_dot(o_ref[:, h * KV_LORA:(h + 1) * KV_LORA], wuv_ref[h]) for h in range(N_HEADS)], axis=1)
    row = lax.broadcasted_iota(jnp.int32, (CHUNK, CHUNK), 0)
    col = lax.broadcasted_iota(jnp.int32, (CHUNK, CHUNK), 1)
    keep = (col <= row) & (row // chunk_rows == col // chunk_rows)
    vb = v.astype(BF16)
    bt = bt_ref[...]
    mixed_rows = []
    for c in range(ROW_TILE // CHUNK):
        groups = []
        for g in range(SGU_GROUPS):
            ws = jnp.where(keep, ws_ref[g], 0.0).astype(BF16)
            vg = vb[c * CHUNK:(c + 1) * CHUNK, g * SGU_GROUP_WIDTH:(g + 1) * SGU_GROUP_WIDTH]
            groups.append(_dot(ws, vg) + bt[:, g:g + 1])
        mixed_rows.append(jnp.concatenate(groups, axis=1))
    mixed = jnp.concatenate(mixed_rows, axis=0)
    merged = ga * attn + gb * (u * mixed)
    x1 = x + _dot(merged.astype(BF16), wo_ref[...])
    x1_ref[...] = x1
    xt = _rms(x1, n2_ref[...])
    xt_hi = xt.astype(BF16)
    xt_lo = (xt - xt_hi.astype(F32)).astype(BF16)
    logits = (_dot(xt_hi, wrh_ref[...]) + _dot(xt_lo, wrh_ref[...]) + _dot(xt_hi, wrl_ref[...])) + br_ref[...]
    e0, e1, w0, w1 = _route(logits)
    lane = lax.broadcasted_iota(jnp.int32, logits.shape, 1)
    route_ref[...] = jnp.where(lane == 0, e0.astype(F32),
                               jnp.where(lane == 1, e1.astype(F32),
                                         jnp.where(lane == 2, w0, jnp.where(lane == 3, w1, 0.0))))


def _post(x, o_lat, ws_eff, bt_eff, chunk_rows, emit_v, p):
    rows = x.shape[0]
    nt = rows // ROW_TILE
    row_spec = lambda w: pl.BlockSpec((ROW_TILE, w), lambda i: (i, 0))
    out_specs = [row_spec(D_MODEL), row_spec(LANES)]
    out_shape = [jax.ShapeDtypeStruct((rows, D_MODEL), F32), jax.ShapeDtypeStruct((rows, LANES), F32)]
    if emit_v:
        out_specs.append(row_spec(D_MODEL))
        out_shape.append(jax.ShapeDtypeStruct((rows, D_MODEL), F32))
    return pl.pallas_call(
        functools.partial(_post_kernel, chunk_rows=chunk_rows),
        grid=(nt,),
        in_specs=[row_spec(D_MODEL), row_spec(N_HEADS * KV_LORA),
                  _const_spec((1, D_MODEL)), _const_spec((D_MODEL, 4 * D_MODEL)), _const_spec((1, D_MODEL)),
                  _const_spec((N_HEADS, KV_LORA, V_DIM)), _const_spec((SGU_GROUPS, CHUNK, CHUNK)),
                  _const_spec((CHUNK, SGU_GROUPS)), _const_spec((D_MODEL, D_MODEL)), _const_spec((1, D_MODEL)),
                  _const_spec((D_MODEL, LANES)), _const_spec((D_MODEL, LANES)), _const_spec((1, LANES))],
        out_specs=out_specs,
        out_shape=out_shape,
        compiler_params=pltpu.CompilerParams(dimension_semantics=("parallel",), vmem_limit_bytes=VMEM_LIMIT),
        name="merge_route",
    )(x, o_lat, p["norm1_g"], p["w_in_big"], p["sgu_norm_g"], p["w_uv_h"], ws_eff, bt_eff, p["w_o"],
      p["norm2_g"], p["w_r_hi"], p["w_r_lo"], p["b_r"])


def _rank_kernel(route_ref, rank_ref, counts_ref, carry_ref):
    @pl.when(pl.program_id(0) == 0)
    def _():
        carry_ref[...] = jnp.zeros(carry_ref.shape, F32)

    r = route_ref[...]
    lane = lax.broadcasted_iota(jnp.int32, r.shape, 1)
    oh0 = jnp.where(lane == r[:, 0:1].astype(jnp.int32), 1.0, 0.0)
    oh1 = jnp.where(lane == r[:, 1:2].astype(jnp.int32), 1.0, 0.0)
    both = oh0 + oh1
    row = lax.broadcasted_iota(jnp.int32, (ROW_TILE, ROW_TILE), 0)
    col = lax.broadcasted_iota(jnp.int32, (ROW_TILE, ROW_TILE), 1)
    before = jnp.where(col < row, 1.0, 0.0).astype(BF16)
    base = carry_ref[...] + _dot(before, both.astype(BF16))
    rank0 = jnp.sum(oh0 * base, axis=-1, keepdims=True)
    rank1 = jnp.sum(oh1 * (base + oh0), axis=-1, keepdims=True)
    rank_ref[...] = jnp.where(lane == 0, rank0, jnp.where(lane == 1, rank1, 0.0))
    carry_ref[...] = carry_ref[...] + jnp.sum(both, axis=0, keepdims=True)
    counts_ref[...] = carry_ref[...]


def _rank(route):
    rows = route.shape[0]
    return pl.pallas_call(
        _rank_kernel,
        grid=(rows // ROW_TILE,),
        in_specs=[pl.BlockSpec((ROW_TILE, LANES), lambda i: (i, 0))],
        out_specs=[pl.BlockSpec((ROW_TILE, LANES), lambda i: (i, 0)), _const_spec((1, LANES))],
        out_shape=[jax.ShapeDtypeStruct((rows, LANES), F32), jax.ShapeDtypeStruct((1, LANES), F32)],
        scratch_shapes=[pltpu.VMEM((1, LANES), F32)],
        compiler_params=pltpu.CompilerParams(dimension_semantics=("arbitrary",)),
        name="expert_rank",
    )(route)


def _row_copy(src, src_row, dst, dst_row, sem):
    return pltpu.make_async_copy(src.at[pl.ds(src_row, 1)], dst.at[pl.ds(dst_row, 1)], sem)


def _dispatch_kernel(pos_ref, x1_ref, n2_ref, xs_in_ref, xs_ref, xt_ref, sems, *, n_tiles):
    del xs_in_ref
    i = pl.program_id(0)
    slot = i % 2
    xt_ref[slot] = _rms(x1_ref[...], n2_ref[...])

    def issue(r, carry):
        for k in range(TOP_K):
            _row_copy(xt_ref.at[slot], r, xs_ref, pos_ref[0, 0, TOP_K * r + k], sems.at[slot]).start()
        return carry

    lax.fori_loop(0, ROW_TILE, issue, 0, unroll=8)

    def drain(s):
        for k in range(TOP_K):
            pltpu.make_async_copy(xt_ref.at[s], xt_ref.at[s], sems.at[s]).wait()

    @pl.when(i > 0)
    def _():
        drain(1 - slot)

    @pl.when(i == n_tiles - 1)
    def _():
        drain(slot)


def _dispatch(pos3, x1, norm2_g, xs):
    rows = x1.shape[0]
    return pl.pallas_call(
        functools.partial(_dispatch_kernel, n_tiles=rows // ROW_TILE),
        grid=(rows // ROW_TILE,),
        in_specs=[pl.BlockSpec((1, 1, TOP_K * ROW_TILE), lambda i: (i, 0, 0), memory_space=pltpu.SMEM),
                  pl.BlockSpec((ROW_TILE, D_MODEL), lambda i: (i, 0)),
                  _const_spec((1, D_MODEL)),
                  pl.BlockSpec(memory_space=pl.ANY)],
        out_specs=pl.BlockSpec(memory_space=pl.ANY),
        out_shape=jax.ShapeDtypeStruct(xs.shape, xs.dtype),
        scratch_shapes=[pltpu.VMEM((2, ROW_TILE, D_MODEL), F32), pltpu.SemaphoreType.DMA((2,))],
        input_output_aliases={3: 0},
        compiler_params=pltpu.CompilerParams(dimension_semantics=("arbitrary",), has_side_effects=True),
        name="dispatch",
    )(pos3, x1, norm2_g, xs)


def _expert_kernel(be_ref, nb_ref, xs_ref, wg_ref, wu_ref, wd_ref, ys_ref, wg_s, wu_s, wd_s):
    i = pl.program_id(0)
    prev = be_ref[jnp.maximum(i - 1, 0)]

    @pl.when((i == 0) | (be_ref[i] != prev))
    def _():
        wg_s[...] = wg_ref[...].astype(BF16)
        wu_s[...] = wu_ref[...].astype(BF16)
        wd_s[...] = wd_ref[...].astype(BF16)

    @pl.when(i < nb_ref[0])
    def _():
        xb = xs_ref[...].astype(BF16)
        h = jax.nn.silu(_dot(xb, wg_s[...])) * _dot(xb, wu_s[...])
        ys_ref[...] = _dot(h.astype(BF16), wd_s[...])

    @pl.when(i >= nb_ref[0])
    def _():
        ys_ref[...] = jnp.zeros(ys_ref.shape, F32)


def _experts(block_expert, n_used, xs, w_gate, w_up, w_down):
    n_blocks = xs.shape[0] // ROW_BLOCK
    grid_spec = pltpu.PrefetchScalarGridSpec(
        num_scalar_prefetch=2,
        grid=(n_blocks,),
        in_specs=[pl.BlockSpec((ROW_BLOCK, D_MODEL), lambda i, be, nb: (i, 0)),
                  pl.BlockSpec((None, D_MODEL, D_EXPERT), lambda i, be, nb: (be[i], 0, 0)),
                  pl.BlockSpec((None, D_MODEL, D_EXPERT), lambda i, be, nb: (be[i], 0, 0)),
                  pl.BlockSpec((None, D_EXPERT, D_MODEL), lambda i, be, nb: (be[i], 0, 0))],
        out_specs=pl.BlockSpec((ROW_BLOCK, D_MODEL), lambda i, be, nb: (i, 0)),
        scratch_shapes=[pltpu.VMEM((D_MODEL, D_EXPERT), BF16), pltpu.VMEM((D_MODEL, D_EXPERT), BF16),
                        pltpu.VMEM((D_EXPERT, D_MODEL), BF16)],
    )
    return pl.pallas_call(
        _expert_kernel,
        grid_spec=grid_spec,
        out_shape=jax.ShapeDtypeStruct(xs.shape, F32),
        compiler_params=pltpu.CompilerParams(dimension_semantics=("arbitrary",), vmem_limit_bytes=VMEM_LIMIT),
        name="expert_ffn",
    )(block_expert, n_used, xs, w_gate, w_up, w_down)


def _combine_kernel(pos_ref, pos_next_ref, x1_ref, route_ref, nf_ref, ys_ref, y_ref, buf_ref, sems, *, n_tiles):
    i = pl.program_id(0)
    slot = i % 2

    def gather(p_ref, s):
        def issue(r, carry):
            for k in range(TOP_K):
                _row_copy(ys_ref, p_ref[0, 0, TOP_K * r + k], buf_ref.at[s, k], r, sems.at[s]).start()
            return carry

        lax.fori_loop(0, ROW_TILE, issue, 0, unroll=8)

    @pl.when(i == 0)
    def _():
        gather(pos_ref, 0)

    @pl.when(i + 1 < n_tiles)
    def _():
        gather(pos_next_ref, 1 - slot)

    pltpu.make_async_copy(buf_ref.at[slot], buf_ref.at[slot], sems.at[slot]).wait()
    route = route_ref[...]
    y = x1_ref[...] + buf_ref[slot, 0] * route[:, 2:3] + buf_ref[slot, 1] * route[:, 3:4]
    y_ref[...] = _rms(y, nf_ref[...])


def _combine(pos3, x1, route, norm_f_g, ys):
    rows = x1.shape[0]
    nt = rows // ROW_TILE
    return pl.pallas_call(
        functools.partial(_combine_kernel, n_tiles=nt),
        grid=(nt,),
        in_specs=[pl.BlockSpec((1, 1, TOP_K * ROW_TILE), lambda i: (i, 0, 0), memory_space=pltpu.SMEM),
                  pl.BlockSpec((1, 1, TOP_K * ROW_TILE), lambda i: (jnp.minimum(i + 1, nt - 1), 0, 0),
                               memory_space=pltpu.SMEM),
                  pl.BlockSpec((ROW_TILE, D_MODEL), lambda i: (i, 0)),
                  pl.BlockSpec((ROW_TILE, LANES), lambda i: (i, 0)),
                  _const_spec((1, D_MODEL)),
                  pl.BlockSpec(memory_space=pl.ANY)],
        out_specs=pl.BlockSpec((ROW_TILE, D_MODEL), lambda i: (i, 0)),
        out_shape=jax.ShapeDtypeStruct((rows, D_MODEL), F32),
        scratch_shapes=[pltpu.VMEM((2, TOP_K, ROW_TILE, D_MODEL), F32), pltpu.SemaphoreType.DMA((2,))],
        compiler_params=pltpu.CompilerParams(dimension_semantics=("arbitrary",)),
        name="combine",
    )(pos3, pos3, x1, route, norm_f_g, ys)


def _prepare_params(norm1_g, w_in, q_norm_g, kv_norm_g, w_uq, w_qr, w_uk, w_uv, sgu_norm_g, w_o, norm2_g,
                    w_rg, b_rg, w_re, b_re):
    def rot(w):
        return jnp.concatenate([-w[..., ROPE_HALF:], w[..., :ROPE_HALF]], axis=-1)

    q0, kv0, kr0 = 0, Q_LORA, Q_LORA + KV_LORA
    big0 = kr0 + ROPE_DIM
    w_kr = w_in[:, kr0:big0]
    w_in_small = jnp.concatenate([w_in[:, q0:kr0], w_kr, w_kr, rot(w_kr), rot(w_kr)], axis=1).astype(BF16)
    zeros = jnp.zeros((Q_LORA, N_HEADS, LANES - ROPE_DIM), w_qr.dtype)
    pad = lambda w: jnp.concatenate([w, zeros], axis=-1).reshape(Q_LORA, N_HEADS * LANES)
    w_qr2 = jnp.concatenate([pad(w_qr), pad(rot(w_qr))], axis=1).astype(BF16)
    w_r = jnp.concatenate([w_rg, w_re, jnp.zeros((D_MODEL, LANES - N_GROUPS - N_EXPERTS), F32)], axis=1)
    w_r_hi = w_r.astype(BF16)
    b_r = jnp.concatenate([b_rg, b_re, jnp.zeros((LANES - N_GROUPS - N_EXPERTS,), F32)])[None, :]
    return dict(
        norm1_g=norm1_g[None, :], w_in_small=w_in_small, w_in_big=w_in[:, big0:].astype(BF16),
        q_norm_g=q_norm_g[None, :], kv_norm_g=kv_norm_g[None, :],
        w_uq=w_uq.reshape(Q_LORA, N_HEADS * NOPE_DIM).astype(BF16), w_qr2=w_qr2,
        w_uk_t=jnp.transpose(w_uk, (1, 2, 0)).astype(BF16), w_uv_h=jnp.transpose(w_uv, (1, 0, 2)).astype(BF16),
        sgu_norm_g=sgu_norm_g[None, :], w_o=w_o.astype(BF16), norm2_g=norm2_g[None, :],
        w_r_hi=w_r_hi, w_r_lo=(w_r - w_r_hi.astype(F32)).astype(BF16), b_r=b_r)


def _moe_and_final_norm(x1s, routes, norm2_g, w_gate, w_up, w_down, norm_f_g):
    route = jnp.concatenate(routes, axis=0)
    rank, counts = _rank(route)
    expert = route[:, :TOP_K].astype(jnp.int32)
    sizes = counts[0, :N_EXPERTS].astype(jnp.int32)
    padded = (sizes + ROW_BLOCK - 1) // ROW_BLOCK * ROW_BLOCK
    pad_end = jnp.cumsum(padded)
    pad_start = pad_end - padded
    start_of = jnp.sum(jnp.where(expert[..., None] == jnp.arange(N_EXPERTS, dtype=jnp.int32), pad_start, 0), axis=-1)
    pos = start_of + rank[:, :TOP_K].astype(jnp.int32)
    n_rows = route.shape[0]
    n_blocks = -(-(n_rows * TOP_K) // ROW_BLOCK) + N_EXPERTS
    block_start = jnp.arange(n_blocks, dtype=jnp.int32) * ROW_BLOCK
    block_expert = jnp.minimum(jnp.sum(pad_end[None, :] <= block_start[:, None], axis=1), N_EXPERTS - 1).astype(jnp.int32)
    n_used = (pad_end[-1:] // ROW_BLOCK).astype(jnp.int32)
    pos3 = pos.reshape(n_rows // ROW_TILE, 1, TOP_K * ROW_TILE)
    tile_ends = [0]
    for x1 in x1s:
        tile_ends.append(tile_ends[-1] + x1.shape[0] // ROW_TILE)
    pos3s = [pos3[a:b] for a, b in zip(tile_ends[:-1], tile_ends[1:])]

    xs = jnp.zeros((n_blocks * ROW_BLOCK, D_MODEL), F32)
    for x1, p3 in zip(x1s, pos3s):
        xs = _dispatch(p3, x1, norm2_g, xs)
    ys = _experts(block_expert, n_used, xs, w_gate, w_up, w_down)
    return [_combine(p3, x1, r, norm_f_g, ys) for x1, r, p3 in zip(x1s, routes, pos3s)]


def kernel(x_prompt, x_sample, cache_ckv, cache_krope, page_table, norm1_g, w_in, q_norm_g, kv_norm_g, w_uq, w_qr,
           w_uk, w_uv, sgu_norm_g, w_s, b_s, w_o, norm2_g, w_rg, b_rg, w_re, b_re, w_gate, w_up, w_down, norm_f_g):
    b_p, s_p, _ = x_prompt.shape
    b_d, t_d, _ = x_sample.shape
    depth = norm1_g.shape[0]
    past_len = page_table.shape[1] * PAGE_SIZE
    rows_p, rows_d = b_p * s_p, b_d * t_d
    assert depth == 1 and s_p % ROW_TILE == 0 and rows_d % ROW_TILE == 0 and CHUNK % t_d == 0
    cos_t, sin_t = _rope_tables(s_p, past_len, t_d)
    tiles_per_seq = s_p // ROW_TILE

    xp = x_prompt.reshape(rows_p, D_MODEL)
    xd = x_sample.reshape(rows_d, D_MODEL)
    l = 0
    p = _prepare_params(norm1_g[l], w_in[l], q_norm_g[l], kv_norm_g[l], w_uq[l], w_qr[l], w_uk[l], w_uv[l],
                        sgu_norm_g[l], w_o[l], norm2_g[l], w_rg[l], b_rg[l], w_re[l], b_re[l])

    ql_p, qr_p, ckv_p, kr_p, ckvb_p, krb_p = _proj(xp, cos_t, sin_t, lambda i: i % tiles_per_seq, p)
    o_p = _prompt_attn(ql_p, qr_p, ckvb_p, krb_p, b_p, s_p)
    bt_p = jnp.transpose(b_s[l])
    x1_p, route_p = _post(xp, o_p, w_s[l], bt_p, CHUNK, False, p)

    ql_d, qr_d, ckv_d, kr_d, ckvb_d, krb_d = _proj(xd, cos_t, sin_t, lambda i: tiles_per_seq, p)
    hr = N_HEADS * t_d
    q_d = ql_d.reshape(b_d, t_d, N_HEADS, KV_LORA).transpose(0, 2, 1, 3).reshape(b_d, hr, KV_LORA)
    qrope_d = qr_d.reshape(b_d, t_d, N_HEADS, LANES)[..., :ROPE_DIM].transpose(0, 2, 1, 3).reshape(b_d, hr, ROPE_DIM)
    pad_rows = lambda a: jnp.pad(a.reshape(b_d, t_d, -1), ((0, 0), (0, PAGE_SIZE - t_d), (0, 0)))
    o_d = _sample_attn(page_table, q_d, qrope_d, pad_rows(ckvb_d), pad_rows(krb_d[:, :ROPE_DIM]),
                       cache_ckv[l], jnp.swapaxes(cache_krope[l], 1, 2), t_d)
    o_d = o_d.reshape(b_d, N_HEADS, t_d, KV_LORA).transpose(0, 2, 1, 3).reshape(rows_d, N_HEADS * KV_LORA)
    reps = CHUNK // t_d
    ws_d = jnp.tile(w_s[l][:, :t_d, :t_d], (1, reps, reps))
    bt_d = jnp.transpose(jnp.tile(b_s[l][:, :t_d], (1, reps)))
    x1_d, route_d, v_d = _post(xd, o_d, ws_d, bt_d, t_d, True, p)

    y_p, y_d = _moe_and_final_norm([x1_p, x1_d], [route_p, route_d], p["norm2_g"], w_gate[l], w_up[l], w_down[l],
                                   norm_f_g[None, :])
    n_pp = s_p // PAGE_SIZE
    return (y_p.reshape(b_p, s_p, D_MODEL),
            y_d.reshape(b_d, t_d, D_MODEL),
            ckv_p.reshape(depth, b_p, n_pp, PAGE_SIZE, KV_LORA),
            kr_p.reshape(depth, b_p, n_pp, PAGE_SIZE, ROPE_DIM),
            ckv_d.reshape(depth, b_d, t_d, KV_LORA),
            kr_d.reshape(depth, b_d, t_d, ROPE_DIM),
            v_d.reshape(depth, b_d, t_d, D_MODEL))
```
